```python
import jax, jax.numpy as jnp
from jax import lax
import numpy as np

D_MODEL = 4096
BATCH = 1
SEQ = 8192
DEPTH = 2

GRID_W = 64
CTX_LEN = 256
NA_HEADS = 16
HEAD_DIM = 128
NA_WIN_H = 8
NA_WIN_W = 16
NA_WIDTH = NA_HEADS * HEAD_DIM
MLA_HEADS = 16
MLA_Q_RANK = 1536
MLA_KV_RANK = 512
MLA_NOPE = 128
MLA_ROPE = 64
MLA_V = 128
MLA_QK = MLA_NOPE + MLA_ROPE
MLA_WIDTH = MLA_HEADS * MLA_V
OFF_NA_K = NA_WIDTH
OFF_Q_LAT = 3 * NA_WIDTH
OFF_KV_LAT = OFF_Q_LAT + MLA_Q_RANK
IN_WIDTH = OFF_KV_LAT + MLA_KV_RANK + MLA_ROPE
CONV_WIDTH = 31
D_FF = 4 * D_MODEL
ROPE_THETA = 10000.0
NORM_EPS = 1e-6
LN_EPS = 1e-5
Q_BLOCK = 128
N_ATT_LAYERS = (DEPTH + 1) // 2
N_CONV_LAYERS = DEPTH // 2

kernel_name = 'hybrid_na_mla_conformer_dit_block'


def rms_norm(x, g):
    xf = x.astype(jnp.float32)
    y = xf * lax.rsqrt(jnp.mean(jnp.square(xf), axis=-1, keepdims=True) + NORM_EPS)
    return (y * g.astype(jnp.float32)).astype(x.dtype)


def layer_norm(x, g, b):
    xf = x.astype(jnp.float32)
    mu = jnp.mean(xf, axis=-1, keepdims=True)
    var = jnp.mean(jnp.square(xf - mu), axis=-1, keepdims=True)
    y = (xf - mu) * lax.rsqrt(var + LN_EPS) * g.astype(jnp.float32) + b.astype(jnp.float32)
    return y.astype(x.dtype)


def modulate(h, shift, scale):
    return h * (1 + scale) + shift


def axial_rope(n_tok):
    t = jnp.arange(n_tok, dtype=jnp.int32)
    pos = jnp.stack([t // GRID_W, t % GRID_W], axis=-1).astype(jnp.float32)
    n_freq = MLA_ROPE // 4
    inv_freq = ROPE_THETA ** (-jnp.arange(n_freq, dtype=jnp.float32) / n_freq)
    ang = (pos[:, :, None] * inv_freq).reshape(n_tok, 2 * n_freq)
    return jnp.cos(ang), jnp.sin(ang)


def rotate_tail(t, rope):
    cos, sin = rope
    c = cos[:, None, :]
    s = sin[:, None, :]
    half = MLA_ROPE // 2
    head, x1, x2 = t[..., :-MLA_ROPE], t[..., -MLA_ROPE:-half], t[..., -half:]
    return jnp.concatenate([head, (x1 * c - x2 * s).astype(t.dtype), (x1 * s + x2 * c).astype(t.dtype)], axis=-1)


def dense_block_attention(q, k, v, scale):
    B, S, H, Dq = q.shape
    qb = jnp.moveaxis(q.reshape(B, S // Q_BLOCK, Q_BLOCK, H, Dq), 1, 0)

    def one_block(q_blk):
        s = jnp.einsum('bqhd,bkhd->bhqk', q_blk, k).astype(jnp.float32) * scale
        p = jax.nn.softmax(s, axis=-1).astype(v.dtype)
        return jnp.einsum('bhqk,bkhd->bqhd', p, v)

    out = lax.map(one_block, qb)
    return jnp.moveaxis(out, 0, 1).reshape(B, S, H, v.shape[-1])


def neighbourhood_attention(q, k, v, k_ctx, v_ctx, rpb):
    B, S, H, Dh = q.shape
    rows = S // GRID_W
    kh = min(NA_WIN_H, rows)
    n_key = kh * GRID_W
    scale = Dh ** -0.5
    qg = q.reshape(B, rows, GRID_W, H, Dh)
    kg = k.reshape(B, rows, GRID_W, H, Dh)
    vg = v.reshape(B, rows, GRID_W, H, Dh)
    q_col = jnp.arange(GRID_W)
    c_start = jnp.clip(q_col - NA_WIN_W // 2, 0, GRID_W - NA_WIN_W)
    k_col = jnp.tile(jnp.arange(GRID_W), kh)
    k_band_row = jnp.repeat(jnp.arange(kh), GRID_W)
    in_window = (k_col[None, :] >= c_start[:, None]) & (k_col[None, :] < c_start[:, None] + NA_WIN_W)
    col_idx = jnp.clip(k_col[None, :] - q_col[:, None], 1 - NA_WIN_W, NA_WIN_W - 1) + (NA_WIN_W - 1)

    def one_row(r):
        r_start = jnp.clip(r - kh // 2, 0, rows - kh)
        q_r = lax.dynamic_index_in_dim(qg, r, axis=1, keepdims=False)
        k_r = lax.dynamic_slice_in_dim(kg, r_start, kh, axis=1).reshape(B, n_key, H, Dh)
        v_r = lax.dynamic_slice_in_dim(vg, r_start, kh, axis=1).reshape(B, n_key, H, Dh)
        row_idx = r_start + k_band_row - r + (NA_WIN_H - 1)
        bias = rpb[:, row_idx[None, :], col_idx].astype(jnp.float32)
        s_win = jnp.einsum('bqhd,bkhd->bhqk', q_r, k_r).astype(jnp.float32) * scale + bias
        s_win = jnp.where(in_window, s_win, -jnp.inf)
        s_ctx = jnp.einsum('bqhd,bchd->bhqc', q_r, k_ctx).astype(jnp.float32) * scale
        p = jax.nn.softmax(jnp.concatenate([s_win, s_ctx], axis=-1), axis=-1).astype(v.dtype)
        return (jnp.einsum('bhqk,bkhd->bqhd', p[..., :n_key], v_r)
                + jnp.einsum('bhqc,bchd->bqhd', p[..., n_key:], v_ctx))

    out = lax.map(one_row, jnp.arange(rows))
    return jnp.moveaxis(out, 0, 1).reshape(B, S, H, Dh)


def mla_heads(q_lat, kv_lat, k_rope, g_qa, w_qb, g_kva, w_kvb, g_q, g_k, rope):
    B, N, _ = kv_lat.shape
    kv = (rms_norm(kv_lat, g_kva) @ w_kvb).reshape(B, N, MLA_HEADS, MLA_NOPE + MLA_V)
    k_nope, v = kv[..., :MLA_NOPE], kv[..., MLA_NOPE:]
    k_pe = jnp.broadcast_to(k_rope[:, :, None, :], (B, N, MLA_HEADS, MLA_ROPE))
    k = rms_norm(jnp.concatenate([k_nope, k_pe], axis=-1), g_k)
    q = None
    if q_lat is not None:
        q = rms_norm((rms_norm(q_lat, g_qa) @ w_qb).reshape(B, N, MLA_HEADS, MLA_QK), g_q)
    if rope is not None:
        k = rotate_tail(k, rope)
        q = rotate_tail(q, rope)
    return q, k, v


def attention_mixer(h_lat, h_ctx, ctx_queries, rope, w_in, g_qa, w_qb, g_kva, w_kvb,
                    g_na_q, g_na_k, na_rpb, g_mla_q, g_mla_k, w_out):
    def project(h, with_q, rope_tab):
        B, N, _ = h.shape
        heads = lambda t: t.reshape(B, N, NA_HEADS, HEAD_DIM)
        if with_q:
            z = h @ w_in
            na_q = rms_norm(heads(z[..., :OFF_NA_K]), g_na_q)
            na_kv = z[..., OFF_NA_K:OFF_Q_LAT]
            q_lat = z[..., OFF_Q_LAT:OFF_KV_LAT]
            kv_rest = z[..., OFF_KV_LAT:]
        else:
            na_q = q_lat = None
            na_kv = h @ w_in[:, OFF_NA_K:OFF_Q_LAT]
            kv_rest = h @ w_in[:, OFF_KV_LAT:]
        na_k = rms_norm(heads(na_kv[..., :NA_WIDTH]), g_na_k)
        na_v = heads(na_kv[..., NA_WIDTH:])
        m_q, m_k, m_v = mla_heads(q_lat, kv_rest[..., :MLA_KV_RANK], kv_rest[..., MLA_KV_RANK:],
                                  g_qa, w_qb, g_kva, w_kvb, g_mla_q, g_mla_k, rope_tab)
        return na_q, na_k, na_v, m_q, m_k, m_v

    B, S, _ = h_lat.shape
    nq, nk, nv, mq, mk, mv = project(h_lat, True, rope)
    cnq, cnk, cnv, cmq, cmk, cmv = project(h_ctx, ctx_queries, None)
    o_na = neighbourhood_attention(nq, nk, nv, cnk, cnv, na_rpb)
    o_mla = dense_block_attention(mq, jnp.concatenate([cmk, mk], axis=1),
                                  jnp.concatenate([cmv, mv], axis=1), MLA_QK ** -0.5)
    y_lat = jnp.concatenate([o_na.reshape(B, S, NA_WIDTH), o_mla.reshape(B, S, MLA_WIDTH)], axis=-1) @ w_out
    y_ctx = None
    if ctx_queries:
        C = h_ctx.shape[1]
        o_cna = dense_block_attention(cnq, cnk, cnv, HEAD_DIM ** -0.5)
        o_cmla = dense_block_attention(cmq, cmk, cmv, MLA_QK ** -0.5)
        y_ctx = jnp.concatenate([o_cna.reshape(B, C, NA_WIDTH), o_cmla.reshape(B, C, MLA_WIDTH)], axis=-1) @ w_out
    return y_lat, y_ctx


def conformer_conv(h, w_pw1, b_pw1, w_dw, b_dw, g_ln, b_ln, w_pw2, b_pw2):
    D = h.shape[-1]
    a, gate = jnp.split(h @ w_pw1 + b_pw1, 2, axis=-1)
    u = a * jax.nn.sigmoid(gate)
    pad = CONV_WIDTH // 2
    u = lax.conv_general_dilated(u, w_dw[:, None, :], window_strides=(1,), padding=[(pad, pad)],
                                 dimension_numbers=('NWC', 'WIO', 'NWC'), feature_group_count=D) + b_dw
    u = layer_norm(u, g_ln, b_ln)
    return jax.nn.silu(u) @ w_pw2 + b_pw2


def squared_relu_mlp(h, w1, w2):
    return jnp.square(jax.nn.relu(h @ w1)) @ w2


def setup_inputs(seed: int = 0) -> dict:
    key = jax.random.key(seed)
    ks = list(jax.random.split(key, 40))
    nrm = lambda shape, s: jax.random.normal(ks.pop(), shape, jnp.float32) * s
    gain = lambda shape: 1.0 + nrm(shape, 0.02)
    D = D_MODEL
    NA = N_ATT_LAYERS
    NC = N_CONV_LAYERS
    return {
        'x': nrm((BATCH, SEQ, D), 1.0),
        'c': nrm((BATCH, D), 1.0),
        'ctx': nrm((BATCH, CTX_LEN, D), 1.0),
        'c_ctx': nrm((D,), 1.0),
        'ada_w': nrm((DEPTH, D, 6 * D), 0.5 * D ** -0.5),
        'ada_b': nrm((DEPTH, 6 * D), 0.01),
        'g_mix': gain((DEPTH, D)),
        'g_mlp': gain((DEPTH, D)),
        'mlp_w1': nrm((DEPTH, D, D_FF), D ** -0.5),
        'mlp_w2': nrm((DEPTH, D_FF, D), D_FF ** -0.5),
        'att_w_in': nrm((NA, D, IN_WIDTH), D ** -0.5),
        'att_g_qa': gain((NA, MLA_Q_RANK)),
        'att_w_qb': nrm((NA, MLA_Q_RANK, MLA_HEADS * MLA_QK), MLA_Q_RANK ** -0.5),
        'att_g_kva': gain((NA, MLA_KV_RANK)),
        'att_w_kvb': nrm((NA, MLA_KV_RANK, MLA_HEADS * (MLA_NOPE + MLA_V)), MLA_KV_RANK ** -0.5),
        'att_g_na_q': gain((NA, HEAD_DIM)),
        'att_g_na_k': gain((NA, HEAD_DIM)),
        'att_na_rpb': nrm((NA, NA_HEADS, 2 * NA_WIN_H - 1, 2 * NA_WIN_W - 1), 0.1),
        'att_g_mla_q': gain((NA, MLA_QK)),
        'att_g_mla_k': gain((NA, MLA_QK)),
        'att_w_out': nrm((NA, NA_WIDTH + MLA_WIDTH, D), (NA_WIDTH + MLA_WIDTH) ** -0.5),
        'conv_w_pw1': nrm((NC, D, 2 * D), D ** -0.5),
        'conv_b_pw1': nrm((NC, 2 * D), 0.01),
        'conv_w_dw': nrm((NC, CONV_WIDTH, D), CONV_WIDTH ** -0.5),
        'conv_b_dw': nrm((NC, D), 0.01),
        'conv_g_ln': gain((NC, D)),
        'conv_b_ln': nrm((NC, D), 0.01),
        'conv_w_pw2': nrm((NC, D, D), D ** -0.5),
        'conv_b_pw2': nrm((NC, D), 0.01),
    }


def reference(x, c, ctx, c_ctx, ada_w, ada_b, g_mix, g_mlp, mlp_w1, mlp_w2,
              att_w_in, att_g_qa, att_w_qb, att_g_kva, att_w_kvb, att_g_na_q, att_g_na_k,
              att_na_rpb, att_g_mla_q, att_g_mla_k, att_w_out,
              conv_w_pw1, conv_b_pw1, conv_w_dw, conv_b_dw, conv_g_ln, conv_b_ln,
              conv_w_pw2, conv_b_pw2):
    rope = axial_rope(x.shape[1])
    sc_lat = jax.nn.silu(c)
    sc_ctx = jax.nn.silu(c_ctx)[None]
    xc = ctx
    for layer in range(DEPTH):
        is_att = layer % 2 == 0
        j = layer // 2
        upd_ctx = any(m % 2 == 0 for m in range(layer + 1, DEPTH))
        mod = jnp.split((sc_lat @ ada_w[layer] + ada_b[layer])[:, None, :], 6, axis=-1)
        hl = modulate(rms_norm(x, g_mix[layer]), mod[0], mod[1])
        if is_att or upd_ctx:
            cmod = jnp.split((sc_ctx @ ada_w[layer] + ada_b[layer])[:, None, :], 6, axis=-1)
            hc = modulate(rms_norm(xc, g_mix[layer]), cmod[0], cmod[1])
        if is_att:
            yl, yc = attention_mixer(hl, hc, upd_ctx, rope, att_w_in[j], att_g_qa[j], att_w_qb[j],
                                     att_g_kva[j], att_w_kvb[j], att_g_na_q[j], att_g_na_k[j],
                                     att_na_rpb[j], att_g_mla_q[j], att_g_mla_k[j], att_w_out[j])
        else:
            conv_p = (conv_w_pw1[j], conv_b_pw1[j], conv_w_dw[j], conv_b_dw[j],
                      conv_g_ln[j], conv_b_ln[j], conv_w_pw2[j], conv_b_pw2[j])
            yl = conformer_conv(hl, *conv_p)
            yc = conformer_conv(hc, *conv_p) if upd_ctx else None
        x = x + mod[2] * yl
        x = x + mod[5] * squared_relu_mlp(modulate(rms_norm(x, g_mlp[layer]), mod[3], mod[4]),
                                          mlp_w1[layer], mlp_w2[layer])
        if upd_ctx:
            xc = xc + cmod[2] * yc
            xc = xc + cmod[5] * squared_relu_mlp(modulate(rms_norm(xc, g_mlp[layer]), cmod[3], cmod[4]),
                                                 mlp_w1[layer], mlp_w2[layer])
    return x
```

```python
import functools
import math

import jax
import jax.numpy as jnp
from jax import lax
from jax.experimental import pallas as pl
from jax.experimental.pallas import tpu as pltpu

F32 = jnp.float32
BF16 = jnp.bfloat16

GRID_W = 64
NA_HEADS = 16
HEAD_DIM = 128
NA_WIN_H = 8
NA_WIN_W = 16
NA_WIDTH = NA_HEADS * HEAD_DIM
MLA_HEADS = 16
MLA_Q_RANK = 1536
MLA_KV_RANK = 512
MLA_NOPE = 128
MLA_ROPE = 64
MLA_V = 128
MLA_QK = MLA_NOPE + MLA_ROPE
OFF_NA_K = NA_WIDTH
OFF_NA_V = 2 * NA_WIDTH
OFF_Q_LAT = 3 * NA_WIDTH
OFF_KV_LAT = OFF_Q_LAT + MLA_Q_RANK
OFF_K_ROPE = OFF_KV_LAT + MLA_KV_RANK
CONV_WIDTH = 31
ROPE_THETA = 10000.0
NORM_EPS = 1e-6
LN_EPS = 1e-5

LANES = 128
V7X_VMEM_LIMIT = 56 * 1024 * 1024
MOD_ROWS = 8
NA_Q_ROWS = 4
NA_K_ROWS = NA_Q_ROWS + NA_WIN_H
NEG_INF = float("-inf")


def _pick(dim, pref, align):
    if dim <= pref:
        return dim
    for t in range(pref - pref % align, 0, -align):
        if dim % t == 0:
            return t
    raise ValueError(f"no tile for {dim}")


def _params(*sem):
    return pltpu.CompilerParams(dimension_semantics=sem, vmem_limit_bytes=V7X_VMEM_LIMIT)


def _mod_body(s_ref, w_ref, b_ref, o_ref):
    s = s_ref[...]
    s = s * jax.nn.sigmoid(s)
    o_ref[0] = jnp.dot(s.astype(BF16), w_ref[0].astype(BF16), preferred_element_type=F32) + b_ref[0]


def _modulation(svec, ada_w, ada_b):
    depth, d, n = ada_w.shape
    tn = _pick(n, 512, LANES)
    return pl.pallas_call(
        _mod_body,
        grid=(depth, n // tn),
        in_specs=[pl.BlockSpec((MOD_ROWS, d), lambda l, j: (0, 0)),
                  pl.BlockSpec((1, d, tn), lambda l, j: (l, 0, j)),
                  pl.BlockSpec((1, 1, tn), lambda l, j: (l, 0, j))],
        out_specs=pl.BlockSpec((1, MOD_ROWS, tn), lambda l, j: (l, 0, j)),
        out_shape=jax.ShapeDtypeStruct((depth, MOD_ROWS, n), F32),
        compiler_params=_params("arbitrary", "arbitrary"),
        name="modulation",
    )(svec, ada_w, ada_b.reshape(depth, 1, n))


def _norm_body(*refs, row, modulated):
    if modulated:
        x_ref, g_ref, shift_ref, scale_ref, o_ref = refs
    else:
        x_ref, g_ref, o_ref = refs
    x = x_ref[...].astype(F32)
    ms = jnp.mean(x * x, axis=-1, keepdims=True)
    y = x * lax.rsqrt(ms + NORM_EPS) * g_ref[...]
    if modulated:
        y = y * (1.0 + scale_ref[0, row:row + 1, :]) + shift_ref[0, row:row + 1, :]
    o_ref[...] = y.astype(o_ref.dtype)


def _rms_norm(x, g, mod=None, layer=0, row=0, shift_chunk=0, scale_chunk=1):
    m, d = x.shape
    tm = _pick(m, 512, 16)
    in_specs = [pl.BlockSpec((tm, d), lambda i: (i, 0)), pl.BlockSpec((1, d), lambda i: (0, 0))]
    args = [x, g.reshape(1, d)]
    if mod is not None:
        in_specs += [pl.BlockSpec((1, MOD_ROWS, d), lambda i: (layer, 0, shift_chunk)),
                     pl.BlockSpec((1, MOD_ROWS, d), lambda i: (layer, 0, scale_chunk))]
        args += [mod, mod]
    return pl.pallas_call(
        functools.partial(_norm_body, row=row, modulated=mod is not None),
        grid=(m // tm,),
        in_specs=in_specs,
        out_specs=pl.BlockSpec((tm, d), lambda i: (i, 0)),
        out_shape=jax.ShapeDtypeStruct((m, d), BF16),
        compiler_params=_params("arbitrary"),
        name="rms_norm",
    )(*args)


def _head_norm(acc, g):
    outs = []
    for c in range(acc.shape[1] // HEAD_DIM):
        blk = acc[:, c * HEAD_DIM:(c + 1) * HEAD_DIM]
        ms = jnp.mean(blk * blk, axis=-1, keepdims=True)
        outs.append(blk * lax.rsqrt(ms + NORM_EPS) * g[:, c * HEAD_DIM:(c + 1) * HEAD_DIM])
    return outs[0] if len(outs) == 1 else jnp.concatenate(outs, axis=-1)


def _epilogue(kind, acc, acc2, ex, gate_row):
    if kind == "plain":
        return acc
    if kind == "relu2":
        r = jnp.maximum(acc, 0.0)
        return r * r
    if kind == "headnorm":
        return _head_norm(acc, ex["gain"][...])
    if kind == "glu":
        b = ex["bias"][...]
        b2 = ex["bias2"][...]
        return (acc + b) * jax.nn.sigmoid(acc2 + b2)
    if kind == "resid":
        y = acc
        if "bias" in ex:
            y = y + ex["bias"][...]
        return ex["res"][...] + ex["gate"][0, gate_row:gate_row + 1, :] * y
    raise ValueError(kind)


def _mm_body(*refs, kind, names, dual, k_steps, gate_row):
    a_ref, w_ref = refs[0], refs[1]
    pos = 2
    w2_ref = None
    if dual:
        w2_ref = refs[pos]
        pos += 1
    ex = dict(zip(names, refs[pos:pos + len(names)]))
    pos += len(names)
    o_ref = refs[pos]
    scratch = refs[pos + 1:]

    a = a_ref[...]
    part = jnp.dot(a, w_ref[...].astype(BF16), preferred_element_type=F32)
    part2 = jnp.dot(a, w2_ref[...].astype(BF16), preferred_element_type=F32) if dual else None

    if k_steps == 1:
        o_ref[...] = _epilogue(kind, part, part2, ex, gate_row).astype(o_ref.dtype)
        return

    acc_ref = scratch[0]
    k = pl.program_id(2)

    @pl.when(k == 0)
    def _():
        acc_ref[...] = part

    @pl.when(k > 0)
    def _():
        acc_ref[...] += part

    @pl.when(k == k_steps - 1)
    def _():
        o_ref[...] = _epilogue(kind, acc_ref[...], None, ex, gate_row).astype(o_ref.dtype)


def _matmul(a, w, *, off=0, n=None, kind="plain", out_dtype=BF16, off2=None,
            gain=None, bias=None, bias2=None, res=None, gate=None, gate_layer=0, gate_row=0,
            gate_chunk=0, tm_pref=1024, tn_pref=512, tk_pref=4096):
    m, kdim = a.shape
    n = w.shape[1] if n is None else n
    tm = _pick(m, tm_pref, 16)
    tn = _pick(n, tn_pref, LANES)
    tk = _pick(kdim, tk_pref, LANES)
    assert off % tn == 0 and (off2 is None or off2 % tn == 0)
    k_steps = kdim // tk
    dual = off2 is not None
    assert not (dual and k_steps > 1)
    ob, ob2 = off // tn, (off2 // tn if dual else 0)

    in_specs = [pl.BlockSpec((tm, tk), lambda i, j, k: (i, k)),
                pl.BlockSpec((tk, tn), lambda i, j, k: (k, ob + j))]
    args = [a, w]
    if dual:
        in_specs.append(pl.BlockSpec((tk, tn), lambda i, j, k: (k, ob2 + j)))
        args.append(w)
    names = []
    for name, vec in (("gain", gain), ("bias", bias), ("bias2", bias2)):
        if vec is not None:
            names.append(name)
            in_specs.append(pl.BlockSpec((1, tn), lambda i, j, k: (0, j)))
            args.append(vec.reshape(1, n).astype(F32))
    if res is not None:
        names.append("res")
        in_specs.append(pl.BlockSpec((tm, tn), lambda i, j, k: (i, j)))
        args.append(res)
    if gate is not None:
        names.append("gate")
        gb = gate_chunk * (n // tn)
        in_specs.append(pl.BlockSpec((1, MOD_ROWS, tn), lambda i, j, k: (gate_layer, 0, gb + j)))
        args.append(gate)

    scratch = [pltpu.VMEM((tm, tn), F32)] if k_steps > 1 else []
    return pl.pallas_call(
        functools.partial(_mm_body, kind=kind, names=tuple(names), dual=dual, k_steps=k_steps,
                          gate_row=gate_row),
        grid=(m // tm, n // tn, k_steps),
        in_specs=in_specs,
        out_specs=pl.BlockSpec((tm, tn), lambda i, j, k: (i, j)),
        out_shape=jax.ShapeDtypeStruct((m, n), out_dtype),
        scratch_shapes=scratch,
        compiler_params=_params("parallel", "parallel", "arbitrary"),
        name="matmul_" + kind,
    )(*args)


def _mla_head_body(nope_ref, rope_ref, gn_ref, gr_ref, cos_ref, sin_ref, o_ref, *, shared_rope,
                   out_scale):
    lane = lax.broadcasted_iota(jnp.int32, (1, LANES), 1)
    first_half = lane < MLA_ROPE // 2
    cos = cos_ref[...]
    sin = sin_ref[...]
    gn = gn_ref[...]
    gr = gr_ref[...]
    for h in range(MLA_HEADS):
        nope = nope_ref[:, h * MLA_NOPE:(h + 1) * MLA_NOPE].astype(F32)
        if shared_rope:
            rope = rope_ref[...].astype(F32)
        else:
            rope = rope_ref[:, h * LANES:(h + 1) * LANES].astype(F32)
        ss = jnp.sum(nope * nope, axis=-1, keepdims=True) + jnp.sum(rope * rope, axis=-1, keepdims=True)
        inv = lax.rsqrt(ss * (1.0 / MLA_QK) + NORM_EPS)
        nope = nope * inv * gn
        rope = rope * inv * gr
        partner = jnp.where(first_half, pltpu.roll(rope, LANES - MLA_ROPE // 2, 1),
                            pltpu.roll(rope, MLA_ROPE // 2, 1))
        rope = rope * cos + partner * sin
        o_ref[h, :, 0:MLA_NOPE] = (nope * out_scale).astype(o_ref.dtype)
        o_ref[h, :, MLA_NOPE:] = (rope * out_scale).astype(o_ref.dtype)


def _mla_heads(nope_src, nope_off, rope_src, rope_off, shared_rope, g, cos, sin, out_scale):
    m = nope_src.shape[0]
    tm = _pick(m, 256, 16)
    g_nope = g[:MLA_NOPE].reshape(1, MLA_NOPE).astype(F32)
    g_rope = jnp.pad(g[MLA_NOPE:], (0, LANES - MLA_ROPE)).reshape(1, LANES).astype(F32)
    nw = MLA_HEADS * MLA_NOPE
    nb = nope_off // nw
    if shared_rope:
        rope_spec = pl.BlockSpec((tm, LANES), lambda i: (i, 0))
    else:
        rw = MLA_HEADS * LANES
        rb = rope_off // rw
        rope_spec = pl.BlockSpec((tm, rw), lambda i: (i, rb))
    return pl.pallas_call(
        functools.partial(_mla_head_body, shared_rope=shared_rope, out_scale=out_scale),
        grid=(m // tm,),
        in_specs=[pl.BlockSpec((tm, nw), lambda i: (i, nb)), rope_spec,
                  pl.BlockSpec((1, MLA_NOPE), lambda i: (0, 0)),
                  pl.BlockSpec((1, LANES), lambda i: (0, 0)),
                  pl.BlockSpec((tm, LANES), lambda i: (i, 0)),
                  pl.BlockSpec((tm, LANES), lambda i: (i, 0))],
        out_specs=pl.BlockSpec((MLA_HEADS, tm, 2 * LANES), lambda i: (0, i, 0)),
        out_shape=jax.ShapeDtypeStruct((MLA_HEADS, m, 2 * LANES), BF16),
        compiler_params=_params("arbitrary"),
        name="mla_head_norm_rope",
    )(nope_src, rope_src, g_nope, g_rope, cos, sin)


def _flash_body(q_ref, k_ref, v_ref, o_ref, m_ref, l_ref, acc_ref, *, kv_steps):
    j = pl.program_id(2)

    @pl.when(j == 0)
    def _():
        m_ref[...] = jnp.full(m_ref.shape, NEG_INF, F32)
        l_ref[...] = jnp.zeros(l_ref.shape, F32)
        acc_ref[...] = jnp.zeros(acc_ref.shape, F32)

    s = lax.dot_general(q_ref[0], k_ref[0], (((1,), (1,)), ((), ())), preferred_element_type=F32)
    m_prev = m_ref[...]
    m_new = jnp.maximum(m_prev, jnp.max(s, axis=-1, keepdims=True))
    alpha = jnp.exp(m_prev - m_new)
    p = jnp.exp(s - m_new)
    l_ref[...] = alpha * l_ref[...] + jnp.sum(p, axis=-1, keepdims=True)
    acc_ref[...] = alpha * acc_ref[...] + jnp.dot(p.astype(BF16), v_ref[...],
                                                  preferred_element_type=F32)
    m_ref[...] = m_new

    @pl.when(j == kv_steps - 1)
    def _():
        o_ref[...] = (acc_ref[...] / l_ref[...]).astype(o_ref.dtype)


def _mla_attention(q, k, v_src, v_off):
    h, s, dq = q.shape
    t = k.shape[1]
    tq = _pick(s, 512, 16)
    tk = _pick(t, 1024, 16)
    vb = v_off // MLA_V
    return pl.pallas_call(
        functools.partial(_flash_body, kv_steps=t // tk),
        grid=(h, s // tq, t // tk),
        in_specs=[pl.BlockSpec((1, tq, dq), lambda hh, i, j: (hh, i, 0)),
                  pl.BlockSpec((1, tk, dq), lambda hh, i, j: (hh, j, 0)),
                  pl.BlockSpec((tk, MLA_V), lambda hh, i, j: (j, vb + hh))],
        out_specs=pl.BlockSpec((tq, MLA_V), lambda hh, i, j: (i, hh)),
        out_shape=jax.ShapeDtypeStruct((s, h * MLA_V), BF16),
        scratch_shapes=[pltpu.VMEM((tq, 1), F32), pltpu.VMEM((tq, 1), F32),
                        pltpu.VMEM((tq, MLA_V), F32)],
        compiler_params=_params("parallel", "parallel", "arbitrary"),
        name="mla_flash_attention",
    )(q, k, v_src)


def _na_variant_rows(rows, variant):
    r0 = (0, NA_Q_ROWS, rows - NA_Q_ROWS)[variant]
    ks = min(max(r0 - NA_WIN_H // 2, 0), rows - NA_K_ROWS)
    return r0, ks


def _na_bias_body(rpb_ref, o_ref, *, rows):
    h = pl.program_id(0)
    pair = 2 * GRID_W
    qc = lax.broadcasted_iota(jnp.int32, (GRID_W, pair), 0)
    lane = lax.broadcasted_iota(jnp.int32, (GRID_W, pair), 1)
    kc = jnp.where(lane < GRID_W, lane, lane - GRID_W)
    c_start = jnp.clip(qc - NA_WIN_W // 2, 0, GRID_W - NA_WIN_W)
    in_cols = (kc >= c_start) & (kc < c_start + NA_WIN_W)
    diag = kc - qc + (NA_WIN_W - 1)
    masked = jnp.full((GRID_W, pair), NEG_INF, F32)
    n_dr = 2 * NA_WIN_H - 1
    tiles = []
    for dr in range(n_dr):
        t = jnp.zeros((GRID_W, pair), F32)
        for jj in range(2 * NA_WIN_W - 1):
            t = jnp.where(diag == jj, rpb_ref[h, dr * (2 * NA_WIN_W - 1) + jj], t)
        tiles.append(jnp.where(in_cols, t, masked))

    def row_tile(variant, a, b):
        r0, ks = _na_variant_rows(rows, variant)
        i, j = r0 + a, ks + b
        r_start = min(max(i - NA_WIN_H // 2, 0), rows - NA_WIN_H)
        if r_start <= j < r_start + NA_WIN_H:
            return tiles[j - i + NA_WIN_H - 1]
        return masked

    left = lane < GRID_W
    for variant in range(3):
        for a in range(NA_Q_ROWS):
            for p in range(NA_K_ROWS // 2):
                t0, t1 = row_tile(variant, a, 2 * p), row_tile(variant, a, 2 * p + 1)
                tile = t0 if t0 is t1 else jnp.where(left, t0, t1)
                o_ref[0, variant, a * GRID_W:(a + 1) * GRID_W, p * pair:(p + 1) * pair] = tile


def _na_bias_table(rpb, rows):
    h = rpb.shape[0]
    nq, nk = NA_Q_ROWS * GRID_W, NA_K_ROWS * GRID_W
    return pl.pallas_call(
        functools.partial(_na_bias_body, rows=rows),
        grid=(h,),
        in_specs=[pl.BlockSpec(memory_space=pltpu.SMEM)],
        out_specs=pl.BlockSpec((1, 3, nq, nk), lambda hh: (hh, 0, 0, 0)),
        out_shape=jax.ShapeDtypeStruct((h, 3, nq, nk), F32),
        compiler_params=_params("arbitrary"),
        name="na_bias_table",
    )(rpb.reshape(h, -1))


def _na_body(q_ref, k_ref, v_ref, kc_ref, vc_ref, b_ref, o_ref, *, rows, scale):
    rb = pl.program_id(1)
    ks = jnp.clip(rb * NA_Q_ROWS - NA_WIN_H // 2, 0, rows - NA_K_ROWS)
    start = pl.multiple_of(ks * GRID_W, GRID_W)
    nk = NA_K_ROWS * GRID_W
    q = q_ref[...]
    k_win = k_ref[pl.ds(start, nk), :]
    v_win = v_ref[pl.ds(start, nk), :]
    nt = (((1,), (1,)), ((), ()))
    s_win = lax.dot_general(q, k_win, nt, preferred_element_type=F32) * scale + b_ref[0, 0]
    s_ctx = lax.dot_general(q, kc_ref[...], nt, preferred_element_type=F32) * scale
    m = jnp.maximum(jnp.max(s_win, axis=-1, keepdims=True), jnp.max(s_ctx, axis=-1, keepdims=True))
    p_win = jnp.exp(s_win - m)
    p_ctx = jnp.exp(s_ctx - m)
    denom = jnp.sum(p_win, axis=-1, keepdims=True) + jnp.sum(p_ctx, axis=-1, keepdims=True)
    acc = jnp.dot(p_win.astype(BF16), v_win, preferred_element_type=F32)
    acc = acc + jnp.dot(p_ctx.astype(BF16), vc_ref[...], preferred_element_type=F32)
    o_ref[...] = (acc / denom).astype(o_ref.dtype)


def _na_attention(qk, v, ck, cv, bias, rows):
    s = qk.shape[0]
    c = ck.shape[0]
    nq, nk = NA_Q_ROWS * GRID_W, NA_K_ROWS * GRID_W
    n_blk = rows // NA_Q_ROWS

    def variant(rb):
        return jnp.where(rb == 0, 0, jnp.where(rb == n_blk - 1, 2, 1))

    return pl.pallas_call(
        functools.partial(_na_body, rows=rows, scale=HEAD_DIM ** -0.5),
        grid=(NA_HEADS, n_blk),
        in_specs=[pl.BlockSpec((nq, HEAD_DIM), lambda hh, rb: (rb, hh)),
                  pl.BlockSpec((s, HEAD_DIM), lambda hh, rb: (0, NA_HEADS + hh)),
                  pl.BlockSpec((s, HEAD_DIM), lambda hh, rb: (0, hh)),
                  pl.BlockSpec((c, HEAD_DIM), lambda hh, rb: (0, hh)),
                  pl.BlockSpec((c, HEAD_DIM), lambda hh, rb: (0, hh)),
                  pl.BlockSpec((1, 1, nq, nk), lambda hh, rb: (hh, variant(rb), 0, 0))],
        out_specs=pl.BlockSpec((nq, HEAD_DIM), lambda hh, rb: (rb, hh)),
        out_shape=jax.ShapeDtypeStruct((s, NA_WIDTH), BF16),
        compiler_params=_params("parallel", "arbitrary"),
        name="neighbourhood_attention",
    )(qk, qk, v, ck, cv, bias)


CONV_HALO = 16


def _dwconv_body(prev_ref, cur_ref, next_ref, w_ref, bdw_ref, g_ref, b_ref, o_ref, win_ref, *, n_blk):
    i = pl.program_id(0)
    tm = cur_ref.shape[0]
    pad = CONV_WIDTH // 2
    prev = prev_ref[...].astype(F32)
    nxt = next_ref[...].astype(F32)
    win_ref[0:CONV_HALO, :] = jnp.where(i > 0, prev, 0.0)
    win_ref[CONV_HALO:CONV_HALO + tm, :] = cur_ref[...].astype(F32)
    win_ref[CONV_HALO + tm:, :] = jnp.where(i < n_blk - 1, nxt, 0.0)
    acc = jnp.zeros(cur_ref.shape, F32)
    for tap in range(CONV_WIDTH):
        lo = CONV_HALO - pad + tap
        acc = acc + win_ref[lo:lo + tm, :] * w_ref[tap:tap + 1, :]
    u = acc + bdw_ref[...]
    mu = jnp.mean(u, axis=-1, keepdims=True)
    d = u - mu
    var = jnp.mean(d * d, axis=-1, keepdims=True)
    y = d * lax.rsqrt(var + LN_EPS) * g_ref[...] + b_ref[...]
    o_ref[...] = (y * jax.nn.sigmoid(y)).astype(o_ref.dtype)


def _dwconv_ln_silu(u, w_dw, b_dw, g_ln, b_ln):
    m, d = u.shape
    tm = _pick(m, 128, CONV_HALO)
    n_blk = m // tm
    per = tm // CONV_HALO
    last_halo = m // CONV_HALO - 1
    vec = lambda v: v.reshape(1, d).astype(F32)
    return pl.pallas_call(
        functools.partial(_dwconv_body, n_blk=n_blk),
        grid=(n_blk,),
        in_specs=[pl.BlockSpec((CONV_HALO, d), lambda i: (jnp.maximum(i * per - 1, 0), 0)),
                  pl.BlockSpec((tm, d), lambda i: (i, 0)),
                  pl.BlockSpec((CONV_HALO, d), lambda i: (jnp.minimum((i + 1) * per, last_halo), 0)),
                  pl.BlockSpec((CONV_WIDTH, d), lambda i: (0, 0)),
                  pl.BlockSpec((1, d), lambda i: (0, 0)),
                  pl.BlockSpec((1, d), lambda i: (0, 0)),
                  pl.BlockSpec((1, d), lambda i: (0, 0))],
        out_specs=pl.BlockSpec((tm, d), lambda i: (i, 0)),
        out_shape=jax.ShapeDtypeStruct((m, d), BF16),
        scratch_shapes=[pltpu.VMEM((tm + 2 * CONV_HALO, d), F32)],
        compiler_params=_params("arbitrary"),
        name="dwconv_ln_silu",
    )(u, u, u, w_dw.astype(F32), vec(b_dw), vec(g_ln), vec(b_ln))


def _rope_tables(n_ctx, n_tok):
    t = jnp.arange(n_tok, dtype=jnp.int32)
    pos = jnp.stack([t // GRID_W, t % GRID_W], axis=-1).astype(F32)
    n_freq = MLA_ROPE // 4
    inv_freq = ROPE_THETA ** (-jnp.arange(n_freq, dtype=F32) / n_freq)
    ang = (pos[:, :, None] * inv_freq).reshape(n_tok, 2 * n_freq)
    cos, sin = jnp.cos(ang), jnp.sin(ang)
    zeros = jnp.zeros((n_tok, LANES - MLA_ROPE), F32)
    cos_lat = jnp.concatenate([cos, cos, zeros], axis=-1)
    sin_lat = jnp.concatenate([-sin, sin, zeros], axis=-1)
    cos_ctx = jnp.concatenate([jnp.ones((n_ctx, MLA_ROPE), F32), jnp.zeros((n_ctx, LANES - MLA_ROPE), F32)], -1)
    sin_ctx = jnp.zeros((n_ctx, LANES), F32)
    return (jnp.concatenate([cos_ctx, cos_lat], 0), jnp.concatenate([sin_ctx, sin_lat], 0))


def _attention_layer(x, xc, mod, layer, j, p):
    s, d = x.shape
    c = xc.shape[0]
    rows = s // GRID_W
    assert s % GRID_W == 0 and rows % NA_Q_ROWS == 0 and rows >= NA_K_ROWS
    w_in = p["att_w_in"][j]

    hl = _rms_norm(x, p["g_mix"][layer], mod, layer, 0, 0, 1)
    hc = _rms_norm(xc, p["g_mix"][layer], mod, layer, 1, 0, 1)

    g_q = jnp.tile(p["att_g_na_q"][j], NA_HEADS)
    g_k = jnp.tile(p["att_g_na_k"][j], NA_HEADS)
    w_kr = jnp.pad(w_in[:, OFF_K_ROPE:], ((0, 0), (0, LANES - MLA_ROPE)))

    qk = _matmul(hl, w_in, off=0, n=2 * NA_WIDTH, kind="headnorm", gain=jnp.concatenate([g_q, g_k]))
    na_v = _matmul(hl, w_in, off=OFF_NA_V, n=NA_WIDTH)
    q_lat = _matmul(hl, w_in, off=OFF_Q_LAT, n=MLA_Q_RANK)
    kv_lat = _matmul(hl, w_in, off=OFF_KV_LAT, n=MLA_KV_RANK)
    k_rope = _matmul(hl, w_kr, tn_pref=LANES)
    c_k = _matmul(hc, w_in, off=OFF_NA_K, n=NA_WIDTH, kind="headnorm", gain=g_k)
    c_v = _matmul(hc, w_in, off=OFF_NA_V, n=NA_WIDTH)
    c_kv_lat = _matmul(hc, w_in, off=OFF_KV_LAT, n=MLA_KV_RANK)
    c_k_rope = _matmul(hc, w_kr, tn_pref=LANES)

    w_qb = p["att_w_qb"][j].reshape(MLA_Q_RANK, MLA_HEADS, MLA_QK)
    w_qb_rope = jnp.pad(w_qb[:, :, MLA_NOPE:], ((0, 0), (0, 0), (0, LANES - MLA_ROPE)))
    w_qb_perm = jnp.concatenate([w_qb[:, :, :MLA_NOPE].reshape(MLA_Q_RANK, -1),
                                 w_qb_rope.reshape(MLA_Q_RANK, -1)], axis=-1)
    w_kvb = p["att_w_kvb"][j].reshape(MLA_KV_RANK, MLA_HEADS, 2, MLA_NOPE)
    w_kvb_perm = jnp.transpose(w_kvb, (0, 2, 1, 3)).reshape(MLA_KV_RANK, -1)

    q_raw = _matmul(_rms_norm(q_lat, p["att_g_qa"][j]), w_qb_perm)
    kv_n = _rms_norm(jnp.concatenate([c_kv_lat, kv_lat], axis=0), p["att_g_kva"][j])
    kv_raw = _matmul(kv_n, w_kvb_perm)
    kr_all = jnp.concatenate([c_k_rope, k_rope], axis=0)

    cos, sin = _rope_tables(c, s)
    q_full = _mla_heads(q_raw, 0, q_raw, MLA_HEADS * MLA_NOPE, False, p["att_g_mla_q"][j],
                        cos[c:], sin[c:], MLA_QK ** -0.5)
    k_full = _mla_heads(kv_raw, 0, kr_all, 0, True, p["att_g_mla_k"][j], cos, sin, 1.0)
    o_mla = _mla_attention(q_full, k_full, kv_raw, MLA_HEADS * MLA_NOPE)

    bias = _na_bias_table(p["att_na_rpb"][j], rows)
    o_na = _na_attention(qk, na_v, c_k, c_v, bias, rows)

    o = jnp.concatenate([o_na, o_mla], axis=-1)
    return _matmul(o, p["att_w_out"][j], kind="resid", out_dtype=F32, res=x, gate=mod,
                   gate_layer=layer, gate_row=0, gate_chunk=2)


def _conv_layer(x, mod, layer, j, p):
    d = x.shape[1]
    hl = _rms_norm(x, p["g_mix"][layer], mod, layer, 0, 0, 1)
    b1 = p["conv_b_pw1"][j]
    u = _matmul(hl, p["conv_w_pw1"][j], off=0, off2=d, n=d, kind="glu", bias=b1[:d], bias2=b1[d:],
                tn_pref=256)
    v = _dwconv_ln_silu(u, p["conv_w_dw"][j], p["conv_b_dw"][j], p["conv_g_ln"][j], p["conv_b_ln"][j])
    return _matmul(v, p["conv_w_pw2"][j], kind="resid", out_dtype=F32, bias=p["conv_b_pw2"][j],
                   res=x, gate=mod, gate_layer=layer, gate_row=0, gate_chunk=2)


def _mlp(x, mod, layer, p):
    h = _rms_norm(x, p["g_mlp"][layer], mod, layer, 0, 3, 4)
    u = _matmul(h, p["mlp_w1"][layer], kind="relu2")
    return _matmul(u, p["mlp_w2"][layer], kind="resid", out_dtype=F32, res=x, gate=mod,
                   gate_layer=layer, gate_row=0, gate_chunk=5, tk_pref=2048)


def kernel(x, c, ctx, c_ctx, ada_w, ada_b, g_mix, g_mlp, mlp_w1, mlp_w2, att_w_in, att_g_qa, att_w_qb, att_g_kva, att_w_kvb, att_g_na_q, att_g_na_k, att_na_rpb, att_g_mla_q, att_g_mla_k, att_w_out, conv_w_pw1, conv_b_pw1, conv_w_dw, conv_b_dw, conv_g_ln, conv_b_ln, conv_w_pw2, conv_b_pw2):
    p = dict(g_mix=g_mix, g_mlp=g_mlp, mlp_w1=mlp_w1, mlp_w2=mlp_w2, att_w_in=att_w_in,
             att_g_qa=att_g_qa, att_w_qb=att_w_qb, att_g_kva=att_g_kva, att_w_kvb=att_w_kvb,
             att_g_na_q=att_g_na_q, att_g_na_k=att_g_na_k, att_na_rpb=att_na_rpb,
             att_g_mla_q=att_g_mla_q, att_g_mla_k=att_g_mla_k, att_w_out=att_w_out,
             conv_w_pw1=conv_w_pw1, conv_b_pw1=conv_b_pw1, conv_w_dw=conv_w_dw, conv_b_dw=conv_b_dw,
             conv_g_ln=conv_g_ln, conv_b_ln=conv_b_ln, conv_w_pw2=conv_w_pw2, conv_b_pw2=conv_b_pw2)
    batch, _, d = x.shape
    depth = ada_w.shape[0]
    assert batch == 1 and c.shape[0] == 1
    svec = jnp.zeros((MOD_ROWS, d), F32).at[0].set(c[0]).at[1].set(c_ctx)
    mod = _modulation(svec, ada_w, ada_b)

    xl = x[0]
    xc = ctx[0]
    for layer in range(depth):
        j = layer // 2
        if layer % 2 == 0:
            assert not any(m % 2 == 0 for m in range(layer + 1, depth))
            xl = _attention_layer(xl, xc, mod, layer, j, p)
        else:
            xl = _conv_layer(xl, mod, layer, j, p)
        xl = _mlp(xl, mod, layer, p)
    return xl[None]
```

```python
import functools
import math

import jax
import jax.numpy as jnp
from jax import lax
from jax.experimental import pallas as pl
from jax.experimental.pallas import tpu as pltpu

F32 = jnp.float32
BF16 = jnp.bfloat16

GRID_W = 64
NA_HEADS = 16
HEAD_DIM = 128
NA_WIN_H = 8
NA_WIN_W = 16
NA_WIDTH = NA_HEADS * HEAD_DIM
MLA_HEADS = 16
MLA_Q_RANK = 1536
MLA_KV_RANK = 512
MLA_NOPE = 128
MLA_ROPE = 64
MLA_V = 128
MLA_QK = MLA_NOPE + MLA_ROPE
OFF_NA_K = NA_WIDTH
OFF_NA_V = 2 * NA_WIDTH
OFF_Q_LAT = 3 * NA_WIDTH
OFF_KV_LAT = OFF_Q_LAT + MLA_Q_RANK
OFF_K_ROPE = OFF_KV_LAT + MLA_KV_RANK
CONV_WIDTH = 31
ROPE_THETA = 10000.0
NORM_EPS = 1e-6
LN_EPS = 1e-5

LANES = 128
V7X_VMEM_LIMIT = 56 * 1024 * 1024
MOD_ROWS = 8
NA_Q_ROWS = 4
NA_K_ROWS = NA_Q_ROWS + NA_WIN_H
NEG_INF = float("-inf")


def _pick(dim, pref, align):
    if dim <= pref:
        return dim
    for t in range(pref - pref % align, 0, -align):
        if dim % t == 0:
            return t
    raise ValueError(f"no tile for {dim}")


def _params(*sem):
    return pltpu.CompilerParams(dimension_semantics=sem, vmem_limit_bytes=V7X_VMEM_LIMIT)


def _mod_body(s_ref, w_ref, b_ref, o_ref):
    s = s_ref[...]
    s = s * jax.nn.sigmoid(s)
    o_ref[0] = jnp.dot(s.astype(BF16), w_ref[0].astype(BF16), preferred_element_type=F32) + b_ref[0]


def _modulation(svec, ada_w, ada_b):
    depth, d, n = ada_w.shape
    tn = _pick(n, 512, LANES)
    return pl.pallas_call(
        _mod_body,
        grid=(depth, n // tn),
        in_specs=[pl.BlockSpec((MOD_ROWS, d), lambda l, j: (0, 0)),
                  pl.BlockSpec((1, d, tn), lambda l, j: (l, 0, j)),
                  pl.BlockSpec((1, 1, tn), lambda l, j: (l, 0, j))],
        out_specs=pl.BlockSpec((1, MOD_ROWS, tn), lambda l, j: (l, 0, j)),
        out_shape=jax.ShapeDtypeStruct((depth, MOD_ROWS, n), F32),
        compiler_params=_params("arbitrary", "arbitrary"),
        name="modulation",
    )(svec, ada_w, ada_b.reshape(depth, 1, n))


def _norm_body(*refs, row, modulated):
    if modulated:
        x_ref, g_ref, shift_ref, scale_ref, o_ref = refs
    else:
        x_ref, g_ref, o_ref = refs
    x = x_ref[...].astype(F32)
    ms = jnp.mean(x * x, axis=-1, keepdims=True)
    y = x * lax.rsqrt(ms + NORM_EPS) * g_ref[...]
    if modulated:
        y = y * (1.0 + scale_ref[0, row:row + 1, :]) + shift_ref[0, row:row + 1, :]
    o_ref[...] = y.astype(o_ref.dtype)


def _rms_norm(x, g, mod=None, layer=0, row=0, shift_chunk=0, scale_chunk=1):
    m, d = x.shape
    tm = _pick(m, 512, 16)
    in_specs = [pl.BlockSpec((tm, d), lambda i: (i, 0)), pl.BlockSpec((1, d), lambda i: (0, 0))]
    args = [x, g.reshape(1, d)]
    if mod is not None:
        in_specs += [pl.BlockSpec((1, MOD_ROWS, d), lambda i: (layer, 0, shift_chunk)),
                     pl.BlockSpec((1, MOD_ROWS, d), lambda i: (layer, 0, scale_chunk))]
        args += [mod, mod]
    return pl.pallas_call(
        functools.partial(_norm_body, row=row, modulated=mod is not None),
        grid=(m // tm,),
        in_specs=in_specs,
        out_specs=pl.BlockSpec((tm, d), lambda i: (i, 0)),
        out_shape=jax.ShapeDtypeStruct((m, d), BF16),
        compiler_params=_params("arbitrary"),
        name="rms_norm",
    )(*args)


def _head_norm(acc, g):
    outs = []
    for c in range(acc.shape[1] // HEAD_DIM):
        blk = acc[:, c * HEAD_DIM:(c + 1) * HEAD_DIM]
        ms = jnp.mean(blk * blk, axis=-1, keepdims=True)
        outs.append(blk * lax.rsqrt(ms + NORM_EPS) * g[:, c * HEAD_DIM:(c + 1) * HEAD_DIM])
    return outs[0] if len(outs) == 1 else jnp.concatenate(outs, axis=-1)


def _epilogue(kind, acc, acc2, ex, gate_row):
    if kind == "plain":
        return acc
    if kind == "relu2":
        r = jnp.maximum(acc, 0.0)
        return r * r
    if kind == "headnorm":
        return _head_norm(acc, ex["gain"][...])
    if kind == "glu":
        b = ex["bias"][...]
        b2 = ex["bias2"][...]
        return (acc + b) * jax.nn.sigmoid(acc2 + b2)
    if kind == "resid":
        y = acc
        if "bias" in ex:
            y = y + ex["bias"][...]
        return ex["res"][...] + ex["gate"][0, gate_row:gate_row + 1, :] * y
    raise ValueError(kind)


def _mm_body(*refs, kind, names, dual, k_steps, gate_row):
    a_ref, w_ref = refs[0], refs[1]
    pos = 2
    w2_ref = None
    if dual:
        w2_ref = refs[pos]
        pos += 1
    ex = dict(zip(names, refs[pos:pos + len(names)]))
    pos += len(names)
    o_ref = refs[pos]
    scratch = refs[pos + 1:]

    a = a_ref[...]
    if k_steps == 1:
        part = jnp.dot(a, w_ref[...].astype(BF16), preferred_element_type=F32)
        part2 = jnp.dot(a, w2_ref[...].astype(BF16), preferred_element_type=F32) if dual else None
        o_ref[...] = _epilogue(kind, part, part2, ex, gate_row).astype(o_ref.dtype)
        return

    acc_ref = scratch[0]
    k = pl.program_id(2)

    @pl.when(k == 0)
    def _():
        acc_ref[...] = jnp.zeros(acc_ref.shape, F32)

    acc_ref[...] += jnp.dot(a, w_ref[...].astype(BF16), preferred_element_type=F32)

    @pl.when(k == k_steps - 1)
    def _():
        o_ref[...] = _epilogue(kind, acc_ref[...], None, ex, gate_row).astype(o_ref.dtype)


def _matmul(a, w, *, off=0, n=None, kind="plain", out_dtype=BF16, off2=None,
            gain=None, bias=None, bias2=None, res=None, gate=None, gate_layer=0, gate_row=0,
            gate_chunk=0, w_layer=None, tm_pref=1024, tn_pref=512, tk_pref=4096):
    m, kdim = a.shape
    n = w.shape[-1] if n is None else n
    tm = _pick(m, tm_pref, 16)
    tn = _pick(n, tn_pref, LANES)
    tk = _pick(kdim, tk_pref, LANES)
    assert off % tn == 0 and (off2 is None or off2 % tn == 0)
    k_steps = kdim // tk
    dual = off2 is not None
    assert not (dual and k_steps > 1)
    ob, ob2 = off // tn, (off2 // tn if dual else 0)

    def w_spec(col0):
        if w_layer is None:
            return pl.BlockSpec((tk, tn), lambda i, j, k: (k, col0 + j))
        return pl.BlockSpec((None, tk, tn), lambda i, j, k: (w_layer, k, col0 + j))

    in_specs = [pl.BlockSpec((tm, tk), lambda i, j, k: (i, k)), w_spec(ob)]
    args = [a, w]
    if dual:
        in_specs.append(w_spec(ob2))
        args.append(w)
    names = []
    for name, vec in (("gain", gain), ("bias", bias), ("bias2", bias2)):
        if vec is not None:
            names.append(name)
            in_specs.append(pl.BlockSpec((1, tn), lambda i, j, k: (0, j)))
            args.append(vec.reshape(1, n).astype(F32))
    if res is not None:
        names.append("res")
        in_specs.append(pl.BlockSpec((tm, tn), lambda i, j, k: (i, j)))
        args.append(res)
    if gate is not None:
        names.append("gate")
        gb = gate_chunk * (n // tn)
        in_specs.append(pl.BlockSpec((1, MOD_ROWS, tn), lambda i, j, k: (gate_layer, 0, gb + j)))
        args.append(gate)

    scratch = [pltpu.VMEM((tm, tn), F32)] if k_steps > 1 else []
    return pl.pallas_call(
        functools.partial(_mm_body, kind=kind, names=tuple(names), dual=dual, k_steps=k_steps,
                          gate_row=gate_row),
        grid=(m // tm, n // tn, k_steps),
        in_specs=in_specs,
        out_specs=pl.BlockSpec((tm, tn), lambda i, j, k: (i, j)),
        out_shape=jax.ShapeDtypeStruct((m, n), out_dtype),
        scratch_shapes=scratch,
        compiler_params=_params("parallel", "parallel", "arbitrary"),
        name="matmul_" + kind,
    )(*args)


def _mla_head_body(nope_ref, rope_ref, gn_ref, gr_ref, cos_ref, sin_ref, o_ref, *, shared_rope,
                   out_scale):
    lane = lax.broadcasted_iota(jnp.int32, (1, LANES), 1)
    first_half = lane < MLA_ROPE // 2
    cos = cos_ref[...]
    sin = sin_ref[...]
    gn = gn_ref[...]
    gr = gr_ref[...]
    for h in range(MLA_HEADS):
        nope = nope_ref[:, h * MLA_NOPE:(h + 1) * MLA_NOPE].astype(F32)
        if shared_rope:
            rope = rope_ref[...].astype(F32)
        else:
            rope = rope_ref[:, h * LANES:(h + 1) * LANES].astype(F32)
        ss = jnp.sum(nope * nope, axis=-1, keepdims=True) + jnp.sum(rope * rope, axis=-1, keepdims=True)
        inv = lax.rsqrt(ss * (1.0 / MLA_QK) + NORM_EPS)
        nope = nope * inv * gn
        rope = rope * inv * gr
        partner = jnp.where(first_half, pltpu.roll(rope, LANES - MLA_ROPE // 2, 1),
                            pltpu.roll(rope, MLA_ROPE // 2, 1))
        rope = rope * cos + partner * sin
        o_ref[h, :, 0:MLA_NOPE] = (nope * out_scale).astype(o_ref.dtype)
        o_ref[h, :, MLA_NOPE:] = (rope * out_scale).astype(o_ref.dtype)


def _mla_heads(nope_src, nope_off, rope_src, rope_off, shared_rope, g, cos, sin, out_scale):
    m = nope_src.shape[0]
    tm = _pick(m, 256, 16)
    g_nope = g[:MLA_NOPE].reshape(1, MLA_NOPE).astype(F32)
    g_rope = jnp.pad(g[MLA_NOPE:], (0, LANES - MLA_ROPE)).reshape(1, LANES).astype(F32)
    nw = MLA_HEADS * MLA_NOPE
    nb = nope_off // nw
    if shared_rope:
        rope_spec = pl.BlockSpec((tm, LANES), lambda i: (i, 0))
    else:
        rw = MLA_HEADS * LANES
        rb = rope_off // rw
        rope_spec = pl.BlockSpec((tm, rw), lambda i: (i, rb))
    return pl.pallas_call(
        functools.partial(_mla_head_body, shared_rope=shared_rope, out_scale=out_scale),
        grid=(m // tm,),
        in_specs=[pl.BlockSpec((tm, nw), lambda i: (i, nb)), rope_spec,
                  pl.BlockSpec((1, MLA_NOPE), lambda i: (0, 0)),
                  pl.BlockSpec((1, LANES), lambda i: (0, 0)),
                  pl.BlockSpec((tm, LANES), lambda i: (i, 0)),
                  pl.BlockSpec((tm, LANES), lambda i: (i, 0))],
        out_specs=pl.BlockSpec((MLA_HEADS, tm, 2 * LANES), lambda i: (0, i, 0)),
        out_shape=jax.ShapeDtypeStruct((MLA_HEADS, m, 2 * LANES), BF16),
        compiler_params=_params("arbitrary"),
        name="mla_head_norm_rope",
    )(nope_src, rope_src, g_nope, g_rope, cos, sin)


FLASH_TQ = 1024
FLASH_TK = 2816
FLASH_SUB = 1
FLASH_ONES_ROWS = 16


def _flash_body(qt_ref, k_ref, vt_ref, o_ref, *, n_chunks, tkc):
    tq = qt_ref.shape[2] // FLASH_SUB
    qts = [qt_ref[0, :, u * tq:(u + 1) * tq] for u in range(FLASH_SUB)]
    m = [jnp.full((1, tq), NEG_INF, F32) for _ in range(FLASH_SUB)]
    acc = [jnp.zeros((vt_ref.shape[2], tq), F32) for _ in range(FLASH_SUB)]
    for c in range(n_chunks):
        k_c = k_ref[0, c * tkc:(c + 1) * tkc, :]
        v_c = vt_ref[0, c]
        for u in range(FLASH_SUB):
            s = jnp.dot(k_c, qts[u], preferred_element_type=F32)
            m_new = jnp.maximum(m[u], jnp.max(s, axis=0, keepdims=True))
            alpha = jnp.exp2(m[u] - m_new)
            p = jnp.exp2(s - m_new).astype(BF16)
            acc[u] = alpha * acc[u] + jnp.dot(v_c, p, preferred_element_type=F32)
            m[u] = m_new
    for u in range(FLASH_SUB):
        out = acc[u][:MLA_V] * (1.0 / acc[u][MLA_V:MLA_V + 1])
        o_ref[u * tq:(u + 1) * tq, :] = out.T.astype(o_ref.dtype)


def _mla_attention(q_t, k, v_t):
    h, dq, s = q_t.shape
    t = k.shape[1]
    n_chunks, dva, tkc = v_t.shape[1:]
    dv = dva - FLASH_ONES_ROWS
    tq = _pick(s, FLASH_TQ, LANES)
    return pl.pallas_call(
        functools.partial(_flash_body, n_chunks=n_chunks, tkc=tkc),
        grid=(h, s // tq),
        in_specs=[pl.BlockSpec((1, dq, tq), lambda hh, i: (hh, 0, i)),
                  pl.BlockSpec((1, t, dq), lambda hh, i: (hh, 0, 0)),
                  pl.BlockSpec((1, n_chunks, dva, tkc), lambda hh, i: (hh, 0, 0, 0))],
        out_specs=pl.BlockSpec((tq, dv), lambda hh, i: (i, hh)),
        out_shape=jax.ShapeDtypeStruct((s, h * dv), BF16),
        compiler_params=_params("parallel", "arbitrary"),
        name="mla_flash_attention",
    )(q_t, k, v_t)


def _na_variant_rows(rows, variant):
    r0 = (0, NA_Q_ROWS, rows - NA_Q_ROWS)[variant]
    ks = min(max(r0 - NA_WIN_H // 2, 0), rows - NA_K_ROWS)
    return r0, ks


def _na_bias_body(rpb_ref, o_ref, *, rows):
    h = pl.program_id(0)
    pair = 2 * GRID_W
    qc = lax.broadcasted_iota(jnp.int32, (GRID_W, pair), 0)
    lane = lax.broadcasted_iota(jnp.int32, (GRID_W, pair), 1)
    kc = jnp.where(lane < GRID_W, lane, lane - GRID_W)
    c_start = jnp.clip(qc - NA_WIN_W // 2, 0, GRID_W - NA_WIN_W)
    in_cols = (kc >= c_start) & (kc < c_start + NA_WIN_W)
    diag = kc - qc + (NA_WIN_W - 1)
    masked = jnp.full((GRID_W, pair), NEG_INF, F32)
    n_dr = 2 * NA_WIN_H - 1
    tiles = []
    for dr in range(n_dr):
        t = jnp.zeros((GRID_W, pair), F32)
        for jj in range(2 * NA_WIN_W - 1):
            t = jnp.where(diag == jj, rpb_ref[h, dr * (2 * NA_WIN_W - 1) + jj], t)
        tiles.append(jnp.where(in_cols, t, masked))

    def row_tile(variant, a, b):
        r0, ks = _na_variant_rows(rows, variant)
        i, j = r0 + a, ks + b
        r_start = min(max(i - NA_WIN_H // 2, 0), rows - NA_WIN_H)
        if r_start <= j < r_start + NA_WIN_H:
            return tiles[j - i + NA_WIN_H - 1]
        return masked

    left = lane < GRID_W
    for variant in range(3):
        for a in range(NA_Q_ROWS):
            for p in range(NA_K_ROWS // 2):
                t0, t1 = row_tile(variant, a, 2 * p), row_tile(variant, a, 2 * p + 1)
                tile = t0 if t0 is t1 else jnp.where(left, t0, t1)
                o_ref[0, variant, a * GRID_W:(a + 1) * GRID_W, p * pair:(p + 1) * pair] = tile


def _na_bias_table(rpb, rows):
    h = rpb.shape[0]
    nq, nk = NA_Q_ROWS * GRID_W, NA_K_ROWS * GRID_W
    return pl.pallas_call(
        functools.partial(_na_bias_body, rows=rows),
        grid=(h,),
        in_specs=[pl.BlockSpec(memory_space=pltpu.SMEM)],
        out_specs=pl.BlockSpec((1, 3, nq, nk), lambda hh: (hh, 0, 0, 0)),
        out_shape=jax.ShapeDtypeStruct((h, 3, nq, nk), F32),
        compiler_params=_params("arbitrary"),
        name="na_bias_table",
    )(rpb.reshape(h, -1))


def _na_body(q_ref, k_ref, v_ref, kc_ref, vc_ref, b_ref, o_ref, *, rows, scale):
    rb = pl.program_id(1)
    ks = jnp.clip(rb * NA_Q_ROWS - NA_WIN_H // 2, 0, rows - NA_K_ROWS)
    start = pl.multiple_of(ks * GRID_W, GRID_W)
    nk = NA_K_ROWS * GRID_W
    q = q_ref[...]
    k_win = k_ref[pl.ds(start, nk), :]
    v_win = v_ref[pl.ds(start, nk), :]
    nt = (((1,), (1,)), ((), ()))
    s_win = lax.dot_general(q, k_win, nt, preferred_element_type=F32) * scale + b_ref[0, 0]
    s_ctx = lax.dot_general(q, kc_ref[...], nt, preferred_element_type=F32) * scale
    m = jnp.maximum(jnp.max(s_win, axis=-1, keepdims=True), jnp.max(s_ctx, axis=-1, keepdims=True))
    p_win = jnp.exp(s_win - m)
    p_ctx = jnp.exp(s_ctx - m)
    denom = jnp.sum(p_win, axis=-1, keepdims=True) + jnp.sum(p_ctx, axis=-1, keepdims=True)
    acc = jnp.dot(p_win.astype(BF16), v_win, preferred_element_type=F32)
    acc = acc + jnp.dot(p_ctx.astype(BF16), vc_ref[...], preferred_element_type=F32)
    o_ref[...] = (acc / denom).astype(o_ref.dtype)


def _na_attention(qk, v, ck, cv, bias, rows):
    s = qk.shape[0]
    c = ck.shape[0]
    nq, nk = NA_Q_ROWS * GRID_W, NA_K_ROWS * GRID_W
    n_blk = rows // NA_Q_ROWS

    def variant(rb):
        return jnp.where(rb == 0, 0, jnp.where(rb == n_blk - 1, 2, 1))

    return pl.pallas_call(
        functools.partial(_na_body, rows=rows, scale=HEAD_DIM ** -0.5),
        grid=(NA_HEADS, n_blk),
        in_specs=[pl.BlockSpec((nq, HEAD_DIM), lambda hh, rb: (rb, hh)),
                  pl.BlockSpec((s, HEAD_DIM), lambda hh, rb: (0, NA_HEADS + hh)),
                  pl.BlockSpec((s, HEAD_DIM), lambda hh, rb: (0, hh)),
                  pl.BlockSpec((c, HEAD_DIM), lambda hh, rb: (0, hh)),
                  pl.BlockSpec((c, HEAD_DIM), lambda hh, rb: (0, hh)),
                  pl.BlockSpec((1, 1, nq, nk), lambda hh, rb: (hh, variant(rb), 0, 0))],
        out_specs=pl.BlockSpec((nq, HEAD_DIM), lambda hh, rb: (rb, hh)),
        out_shape=jax.ShapeDtypeStruct((s, NA_WIDTH), BF16),
        compiler_params=_params("parallel", "arbitrary"),
        name="neighbourhood_attention",
    )(qk, qk, v, ck, cv, bias)


CONV_HALO = 16


SUBLANES = 8


def _dwconv_body(prev_ref, cur_ref, next_ref, w_ref, bdw_ref, g_ref, b_ref, o_ref, win_ref, conv_ref,
                 *, n_blk):
    i = pl.program_id(0)
    tm, d = cur_ref.shape
    nc = d // LANES
    has_prev = i > 0
    has_next = i < n_blk - 1
    for c in range(nc):
        sl = slice(c * LANES, (c + 1) * LANES)
        win_ref[c, 0:CONV_HALO, :] = jnp.where(has_prev, prev_ref[:, sl].astype(F32), 0.0)
        win_ref[c, CONV_HALO:CONV_HALO + tm, :] = cur_ref[:, sl].astype(F32)
        win_ref[c, CONV_HALO + tm:, :] = jnp.where(has_next, next_ref[:, sl].astype(F32), 0.0)

    groups = (CONV_WIDTH + SUBLANES) // SUBLANES
    span = tm + SUBLANES * (groups - 1)

    def chunk(c, carry):
        w = w_ref[c]
        acc = jnp.zeros((tm, LANES), F32) + bdw_ref[c]
        for b in range(SUBLANES):
            shifted = win_ref[c, pl.ds(b, span), :]
            for a in range(groups):
                o = SUBLANES * a + b
                if 1 <= o <= CONV_WIDTH:
                    acc = acc + shifted[SUBLANES * a:SUBLANES * a + tm, :] * w[o - 1:o, :]
        conv_ref[c] = acc
        return carry

    lax.fori_loop(0, nc, chunk, 0)

    total = conv_ref[0]
    for c in range(1, nc):
        total = total + conv_ref[c]
    mu = jnp.sum(total, axis=-1, keepdims=True) * (1.0 / d)
    sq = jnp.zeros((tm, LANES), F32)
    for c in range(nc):
        dlt = conv_ref[c] - mu
        sq = sq + dlt * dlt
    rstd = lax.rsqrt(jnp.sum(sq, axis=-1, keepdims=True) * (1.0 / d) + LN_EPS)
    for c in range(nc):
        y = (conv_ref[c] - mu) * rstd * g_ref[c] + b_ref[c]
        o_ref[:, c * LANES:(c + 1) * LANES] = (y * jax.nn.sigmoid(y)).astype(o_ref.dtype)


def _dwconv_ln_silu(u, w_dw, b_dw, g_ln, b_ln):
    m, d = u.shape
    tm = _pick(m, 128, CONV_HALO)
    nc = d // LANES
    n_blk = m // tm
    per = tm // CONV_HALO
    last_halo = m // CONV_HALO - 1
    assert CONV_WIDTH // 2 < CONV_HALO and CONV_WIDTH + 1 <= 2 * CONV_HALO
    vec = lambda v: v.reshape(nc, 1, LANES).astype(F32)
    w_chunks = jnp.transpose(w_dw.astype(F32).reshape(CONV_WIDTH, nc, LANES), (1, 0, 2))
    whole = lambda shape: pl.BlockSpec(shape, lambda i: (0,) * len(shape))
    return pl.pallas_call(
        functools.partial(_dwconv_body, n_blk=n_blk),
        grid=(n_blk,),
        in_specs=[pl.BlockSpec((CONV_HALO, d), lambda i: (jnp.maximum(i * per - 1, 0), 0)),
                  pl.BlockSpec((tm, d), lambda i: (i, 0)),
                  pl.BlockSpec((CONV_HALO, d), lambda i: (jnp.minimum((i + 1) * per, last_halo), 0)),
                  whole((nc, CONV_WIDTH, LANES)), whole((nc, 1, LANES)), whole((nc, 1, LANES)),
                  whole((nc, 1, LANES))],
        out_specs=pl.BlockSpec((tm, d), lambda i: (i, 0)),
        out_shape=jax.ShapeDtypeStruct((m, d), BF16),
        scratch_shapes=[pltpu.VMEM((nc, tm + 2 * CONV_HALO, LANES), F32),
                        pltpu.VMEM((nc, tm, LANES), F32)],
        compiler_params=_params("arbitrary"),
        name="dwconv_ln_silu",
    )(u, u, u, w_chunks, vec(b_dw), vec(g_ln), vec(b_ln))


def _rope_tables(n_ctx, n_tok):
    t = jnp.arange(n_tok, dtype=jnp.int32)
    pos = jnp.stack([t // GRID_W, t % GRID_W], axis=-1).astype(F32)
    n_freq = MLA_ROPE // 4
    inv_freq = ROPE_THETA ** (-jnp.arange(n_freq, dtype=F32) / n_freq)
    ang = (pos[:, :, None] * inv_freq).reshape(n_tok, 2 * n_freq)
    cos, sin = jnp.cos(ang), jnp.sin(ang)
    zeros = jnp.zeros((n_tok, LANES - MLA_ROPE), F32)
    cos_lat = jnp.concatenate([cos, cos, zeros], axis=-1)
    sin_lat = jnp.concatenate([-sin, sin, zeros], axis=-1)
    cos_ctx = jnp.concatenate([jnp.ones((n_ctx, MLA_ROPE), F32), jnp.zeros((n_ctx, LANES - MLA_ROPE), F32)], -1)
    sin_ctx = jnp.zeros((n_ctx, LANES), F32)
    return (jnp.concatenate([cos_ctx, cos_lat], 0), jnp.concatenate([sin_ctx, sin_lat], 0))


def _attention_layer(x, xc, mod, layer, j, p):
    s, d = x.shape
    c = xc.shape[0]
    rows = s // GRID_W
    assert s % GRID_W == 0 and rows % NA_Q_ROWS == 0 and rows >= NA_K_ROWS
    w_in = p["att_w_in"]

    hl = _rms_norm(x, p["g_mix"][layer], mod, layer, 0, 0, 1)
    hc = _rms_norm(xc, p["g_mix"][layer], mod, layer, 1, 0, 1)

    g_q = jnp.tile(p["att_g_na_q"][j], NA_HEADS)
    g_k = jnp.tile(p["att_g_na_k"][j], NA_HEADS)
    w_kr = jnp.pad(w_in[j, :, OFF_K_ROPE:], ((0, 0), (0, LANES - MLA_ROPE)))

    qk = _matmul(hl, w_in, w_layer=j, off=0, n=2 * NA_WIDTH, kind="headnorm",
                 gain=jnp.concatenate([g_q, g_k]))
    na_v = _matmul(hl, w_in, w_layer=j, off=OFF_NA_V, n=NA_WIDTH)
    q_lat = _matmul(hl, w_in, w_layer=j, off=OFF_Q_LAT, n=MLA_Q_RANK)
    kv_lat = _matmul(hl, w_in, w_layer=j, off=OFF_KV_LAT, n=MLA_KV_RANK)
    k_rope = _matmul(hl, w_kr, tn_pref=LANES)
    c_k = _matmul(hc, w_in, w_layer=j, off=OFF_NA_K, n=NA_WIDTH, kind="headnorm", gain=g_k)
    c_v = _matmul(hc, w_in, w_layer=j, off=OFF_NA_V, n=NA_WIDTH)
    c_kv_lat = _matmul(hc, w_in, w_layer=j, off=OFF_KV_LAT, n=MLA_KV_RANK)
    c_k_rope = _matmul(hc, w_kr, tn_pref=LANES)

    w_qb = p["att_w_qb"][j].reshape(MLA_Q_RANK, MLA_HEADS, MLA_QK)
    w_qb_rope = jnp.pad(w_qb[:, :, MLA_NOPE:], ((0, 0), (0, 0), (0, LANES - MLA_ROPE)))
    w_qb_perm = jnp.concatenate([w_qb[:, :, :MLA_NOPE].reshape(MLA_Q_RANK, -1),
                                 w_qb_rope.reshape(MLA_Q_RANK, -1)], axis=-1)
    w_kvb = p["att_w_kvb"][j].reshape(MLA_KV_RANK, MLA_HEADS, 2, MLA_NOPE)
    w_kvb_perm = jnp.transpose(w_kvb, (0, 2, 1, 3)).reshape(MLA_KV_RANK, -1)

    q_raw = _matmul(_rms_norm(q_lat, p["att_g_qa"][j]), w_qb_perm)
    kv_n = _rms_norm(jnp.concatenate([c_kv_lat, kv_lat], axis=0), p["att_g_kva"][j])
    kv_raw = _matmul(kv_n, w_kvb_perm)
    kr_all = jnp.concatenate([c_k_rope, k_rope], axis=0)

    cos, sin = _rope_tables(c, s)
    q_full = _mla_heads(q_raw, 0, q_raw, MLA_HEADS * MLA_NOPE, False, p["att_g_mla_q"][j],
                        cos[c:], sin[c:], MLA_QK ** -0.5 * math.log2(math.e))
    k_full = _mla_heads(kv_raw, 0, kr_all, 0, True, p["att_g_mla_k"][j], cos, sin, 1.0)
    tkc = _pick(c + s, FLASH_TK, LANES)
    n_chunks = (c + s) // tkc
    v_t = kv_raw[:, MLA_HEADS * MLA_NOPE:].reshape(n_chunks, tkc, MLA_HEADS, MLA_V)
    ones_rows = jnp.zeros((MLA_HEADS, n_chunks, FLASH_ONES_ROWS, tkc), BF16).at[:, :, 0, :].set(1.0)
    v_t = jnp.concatenate([jnp.transpose(v_t, (2, 0, 3, 1)), ones_rows], axis=2)
    o_mla = _mla_attention(jnp.swapaxes(q_full, 1, 2), k_full, v_t)

    bias = _na_bias_table(p["att_na_rpb"][j], rows)
    o_na = _na_attention(qk, na_v, c_k, c_v, bias, rows)

    o = jnp.concatenate([o_na, o_mla], axis=-1)
    return _matmul(o, p["att_w_out"], w_layer=j, kind="resid", out_dtype=F32, res=x, gate=mod,
                   gate_layer=layer, gate_row=0, gate_chunk=2)


def _conv_layer(x, mod, layer, j, p):
    d = x.shape[1]
    hl = _rms_norm(x, p["g_mix"][layer], mod, layer, 0, 0, 1)
    b1 = p["conv_b_pw1"][j]
    u = _matmul(hl, p["conv_w_pw1"], w_layer=j, off=0, off2=d, n=d, kind="glu", bias=b1[:d],
                bias2=b1[d:], tn_pref=256)
    v = _dwconv_ln_silu(u, p["conv_w_dw"][j], p["conv_b_dw"][j], p["conv_g_ln"][j], p["conv_b_ln"][j])
    return _matmul(v, p["conv_w_pw2"], w_layer=j, kind="resid", out_dtype=F32, bias=p["conv_b_pw2"][j],
                   res=x, gate=mod, gate_layer=layer, gate_row=0, gate_chunk=2)


def _mlp(x, mod, layer, p):
    h = _rms_norm(x, p["g_mlp"][layer], mod, layer, 0, 3, 4)
    u = _matmul(h, p["mlp_w1"], w_layer=layer, kind="relu2")
    return _matmul(u, p["mlp_w2"], w_layer=layer, kind="resid", out_dtype=F32, res=x, gate=mod,
                   gate_layer=layer, gate_row=0, gate_chunk=5, tm_pref=2048, tn_pref=1024, tk_pref=512)


def kernel(x, c, ctx, c_ctx, ada_w, ada_b, g_mix, g_mlp, mlp_w1, mlp_w2, att_w_in, att_g_qa, att_w_qb, att_g_kva, att_w_kvb, att_g_na_q, att_g_na_k, att_na_rpb, att_g_mla_q, att_g_mla_k, att_w_out, conv_w_pw1, conv_b_pw1, conv_w_dw, conv_b_dw, conv_g_ln, conv_b_ln, conv_w_pw2, conv_b_pw2):
    p = dict(g_mix=g_mix, g_mlp=g_mlp, mlp_w1=mlp_w1, mlp_w2=mlp_w2, att_w_in=att_w_in,
             att_g_qa=att_g_qa, att_w_qb=att_w_qb, att_g_kva=att_g_kva, att_w_kvb=att_w_kvb,
             att_g_na_q=att_g_na_q, att_g_na_k=att_g_na_k, att_na_rpb=att_na_rpb,
             att_g_mla_q=att_g_mla_q, att_g_mla_k=att_g_mla_k, att_w_out=att_w_out,
             conv_w_pw1=conv_w_pw1, conv_b_pw1=conv_b_pw1, conv_w_dw=conv_w_dw, conv_b_dw=conv_b_dw,
             conv_g_ln=conv_g_ln, conv_b_ln=conv_b_ln, conv_w_pw2=conv_w_pw2, conv_b_pw2=conv_b_pw2)
    batch, _, d = x.shape
    depth = ada_w.shape[0]
    assert batch == 1 and c.shape[0] == 1
    svec = jnp.zeros((MOD_ROWS, d), F32).at[0].set(c[0]).at[1].set(c_ctx)
    mod = _modulation(svec, ada_w, ada_b)

    xl = x[0]
    xc = ctx[0]
    for layer in range(depth):
        j = layer // 2
        if layer % 2 == 0:
            assert not any(m % 2 == 0 for m in range(layer + 1, depth))
            xl = _attention_layer(xl, xc, mod, layer, j, p)
        else:
            xl = _conv_layer(xl, mod, layer, j, p)
        xl = _mlp(xl, mod, layer, p)
    return xl[None]
```

```python
import functools
import math

import jax
import jax.numpy as jnp
from jax import lax
from jax.experimental import pallas as pl
from jax.experimental.pallas import tpu as pltpu

F32 = jnp.float32
BF16 = jnp.bfloat16

GRID_W = 64
NA_HEADS = 16
HEAD_DIM = 128
NA_WIN_H = 8
NA_WIN_W = 16
NA_WIDTH = NA_HEADS * HEAD_DIM
MLA_HEADS = 16
MLA_Q_RANK = 1536
MLA_KV_RANK = 512
MLA_NOPE = 128
MLA_ROPE = 64
MLA_V = 128
MLA_QK = MLA_NOPE + MLA_ROPE
OFF_NA_K = NA_WIDTH
OFF_NA_V = 2 * NA_WIDTH
OFF_Q_LAT = 3 * NA_WIDTH
OFF_KV_LAT = OFF_Q_LAT + MLA_Q_RANK
OFF_K_ROPE = OFF_KV_LAT + MLA_KV_RANK
CONV_WIDTH = 31
ROPE_THETA = 10000.0
NORM_EPS = 1e-6
LN_EPS = 1e-5

LANES = 128
V7X_VMEM_LIMIT = 56 * 1024 * 1024
MOD_ROWS = 8
NA_Q_ROWS = 8
NA_K_ROWS = NA_Q_ROWS + NA_WIN_H
NA_V_CHUNK = 256
NA_HEADS_PER_STEP = 2
NEG_INF = float("-inf")
LOG2E = math.log2(math.e)


def _pick(dim, pref, align):
    if dim <= pref:
        return dim
    for t in range(pref - pref % align, 0, -align):
        if dim % t == 0:
            return t
    raise ValueError(f"no tile for {dim}")


def _params(*sem):
    return pltpu.CompilerParams(dimension_semantics=sem, vmem_limit_bytes=V7X_VMEM_LIMIT)


def _mod_body(s_ref, w_ref, b_ref, o_ref):
    s = s_ref[...]
    s = s * jax.nn.sigmoid(s)
    o_ref[0] = jnp.dot(s.astype(BF16), w_ref[0].astype(BF16), preferred_element_type=F32) + b_ref[0]


def _modulation(svec, ada_w, ada_b):
    depth, d, n = ada_w.shape
    tn = _pick(n, 512, LANES)
    return pl.pallas_call(
        _mod_body,
        grid=(depth, n // tn),
        in_specs=[pl.BlockSpec((MOD_ROWS, d), lambda l, j: (0, 0)),
                  pl.BlockSpec((1, d, tn), lambda l, j: (l, 0, j)),
                  pl.BlockSpec((1, 1, tn), lambda l, j: (l, 0, j))],
        out_specs=pl.BlockSpec((1, MOD_ROWS, tn), lambda l, j: (l, 0, j)),
        out_shape=jax.ShapeDtypeStruct((depth, MOD_ROWS, n), F32),
        compiler_params=_params("arbitrary", "arbitrary"),
        name="modulation",
    )(svec, ada_w, ada_b.reshape(depth, 1, n))


def _norm_body(*refs, row, modulated):
    if modulated:
        x_ref, g_ref, shift_ref, scale_ref, o_ref = refs
    else:
        x_ref, g_ref, o_ref = refs
    x = x_ref[...].astype(F32)
    ms = jnp.mean(x * x, axis=-1, keepdims=True)
    y = x * lax.rsqrt(ms + NORM_EPS) * g_ref[...]
    if modulated:
        y = y * (1.0 + scale_ref[0, row:row + 1, :]) + shift_ref[0, row:row + 1, :]
    o_ref[...] = y.astype(o_ref.dtype)


def _rms_norm(x, g, mod=None, layer=0, row=0, shift_chunk=0, scale_chunk=1):
    m, d = x.shape
    tm = _pick(m, 512, 16)
    in_specs = [pl.BlockSpec((tm, d), lambda i: (i, 0)), pl.BlockSpec((1, d), lambda i: (0, 0))]
    args = [x, g.reshape(1, d)]
    if mod is not None:
        in_specs += [pl.BlockSpec((1, MOD_ROWS, d), lambda i: (layer, 0, shift_chunk)),
                     pl.BlockSpec((1, MOD_ROWS, d), lambda i: (layer, 0, scale_chunk))]
        args += [mod, mod]
    return pl.pallas_call(
        functools.partial(_norm_body, row=row, modulated=mod is not None),
        grid=(m // tm,),
        in_specs=in_specs,
        out_specs=pl.BlockSpec((tm, d), lambda i: (i, 0)),
        out_shape=jax.ShapeDtypeStruct((m, d), BF16),
        compiler_params=_params("arbitrary"),
        name="rms_norm",
    )(*args)


def _head_norm(acc, g):
    outs = []
    for c in range(acc.shape[1] // HEAD_DIM):
        blk = acc[:, c * HEAD_DIM:(c + 1) * HEAD_DIM]
        ms = jnp.mean(blk * blk, axis=-1, keepdims=True)
        outs.append(blk * lax.rsqrt(ms + NORM_EPS) * g[:, c * HEAD_DIM:(c + 1) * HEAD_DIM])
    return outs[0] if len(outs) == 1 else jnp.concatenate(outs, axis=-1)


def _epilogue(kind, acc, acc2, ex, gate_row):
    if kind == "plain":
        return acc
    if kind == "relu2":
        r = jnp.maximum(acc, 0.0)
        return r * r
    if kind == "headnorm":
        return _head_norm(acc, ex["gain"][...])
    if kind == "glu":
        b = ex["bias"][...]
        b2 = ex["bias2"][...]
        return (acc + b) * jax.nn.sigmoid(acc2 + b2)
    if kind == "resid":
        y = acc
        if "bias" in ex:
            y = y + ex["bias"][...]
        return ex["res"][...] + ex["gate"][0, gate_row:gate_row + 1, :] * y
    raise ValueError(kind)


def _mm_body(*refs, kind, names, dual, k_steps, gate_row):
    a_ref, w_ref = refs[0], refs[1]
    pos = 2
    w2_ref = None
    if dual:
        w2_ref = refs[pos]
        pos += 1
    ex = dict(zip(names, refs[pos:pos + len(names)]))
    pos += len(names)
    o_ref = refs[pos]
    scratch = refs[pos + 1:]

    a = a_ref[...]
    if k_steps == 1:
        part = jnp.dot(a, w_ref[...].astype(BF16), preferred_element_type=F32)
        part2 = jnp.dot(a, w2_ref[...].astype(BF16), preferred_element_type=F32) if dual else None
        o_ref[...] = _epilogue(kind, part, part2, ex, gate_row).astype(o_ref.dtype)
        return

    acc_ref = scratch[0] if scratch else o_ref
    k = pl.program_id(2)

    @pl.when(k == 0)
    def _():
        acc_ref[...] = jnp.zeros(acc_ref.shape, F32)

    acc_ref[...] += jnp.dot(a, w_ref[...].astype(BF16), preferred_element_type=F32)

    @pl.when(k == k_steps - 1)
    def _():
        o_ref[...] = _epilogue(kind, acc_ref[...], None, ex, gate_row).astype(o_ref.dtype)


def _matmul(a, w, *, off=0, n=None, kind="plain", out_dtype=BF16, off2=None,
            gain=None, bias=None, bias2=None, res=None, gate=None, gate_layer=0, gate_row=0,
            gate_chunk=0, w_layer=None, tm_pref=1024, tn_pref=512, tk_pref=4096):
    m, kdim = a.shape
    n = w.shape[-1] if n is None else n
    tm = _pick(m, tm_pref, 16)
    tn = _pick(n, tn_pref, LANES)
    tk = _pick(kdim, tk_pref, LANES)
    assert off % tn == 0 and (off2 is None or off2 % tn == 0)
    k_steps = kdim // tk
    dual = off2 is not None
    assert not (dual and k_steps > 1)
    ob, ob2 = off // tn, (off2 // tn if dual else 0)

    def w_spec(col0):
        if w_layer is None:
            return pl.BlockSpec((tk, tn), lambda i, j, k: (k, col0 + j))
        return pl.BlockSpec((None, tk, tn), lambda i, j, k: (w_layer, k, col0 + j))

    in_specs = [pl.BlockSpec((tm, tk), lambda i, j, k: (i, k)), w_spec(ob)]
    args = [a, w]
    if dual:
        in_specs.append(w_spec(ob2))
        args.append(w)
    names = []
    for name, vec in (("gain", gain), ("bias", bias), ("bias2", bias2)):
        if vec is not None:
            names.append(name)
            in_specs.append(pl.BlockSpec((1, tn), lambda i, j, k: (0, j)))
            args.append(vec.reshape(1, n).astype(F32))
    if res is not None:
        names.append("res")
        in_specs.append(pl.BlockSpec((tm, tn), lambda i, j, k: (i, j)))
        args.append(res)
    if gate is not None:
        names.append("gate")
        gb = gate_chunk * (n // tn)
        in_specs.append(pl.BlockSpec((1, MOD_ROWS, tn), lambda i, j, k: (gate_layer, 0, gb + j)))
        args.append(gate)

    scratch = [pltpu.VMEM((tm, tn), F32)] if k_steps > 1 and out_dtype != F32 else []
    return pl.pallas_call(
        functools.partial(_mm_body, kind=kind, names=tuple(names), dual=dual, k_steps=k_steps,
                          gate_row=gate_row),
        grid=(m // tm, n // tn, k_steps),
        in_specs=in_specs,
        out_specs=pl.BlockSpec((tm, tn), lambda i, j, k: (i, j)),
        out_shape=jax.ShapeDtypeStruct((m, n), out_dtype),
        scratch_shapes=scratch,
        compiler_params=_params("parallel", "parallel", "arbitrary"),
        name="matmul_" + kind,
    )(*args)


def _mla_head_body(nope_ref, rope_ref, gn_ref, gr_ref, cos_ref, sin_ref, o_ref, *, shared_rope,
                   out_scale):
    lane = lax.broadcasted_iota(jnp.int32, (1, LANES), 1)
    first_half = lane < MLA_ROPE // 2
    cos = cos_ref[...]
    sin = sin_ref[...]
    gn = gn_ref[...]
    gr = gr_ref[...]
    for h in range(MLA_HEADS):
        nope = nope_ref[:, h * MLA_NOPE:(h + 1) * MLA_NOPE].astype(F32)
        if shared_rope:
            rope = rope_ref[...].astype(F32)
        else:
            rope = rope_ref[:, h * LANES:(h + 1) * LANES].astype(F32)
        ss = jnp.sum(nope * nope, axis=-1, keepdims=True) + jnp.sum(rope * rope, axis=-1, keepdims=True)
        inv = lax.rsqrt(ss * (1.0 / MLA_QK) + NORM_EPS)
        nope = nope * inv * gn
        rope = rope * inv * gr
        partner = jnp.where(first_half, pltpu.roll(rope, LANES - MLA_ROPE // 2, 1),
                            pltpu.roll(rope, MLA_ROPE // 2, 1))
        rope = rope * cos + partner * sin
        o_ref[h, :, 0:MLA_NOPE] = (nope * out_scale).astype(o_ref.dtype)
        o_ref[h, :, MLA_NOPE:] = (rope * out_scale).astype(o_ref.dtype)


def _mla_heads(nope_src, nope_off, rope_src, rope_off, shared_rope, g, cos, sin, out_scale):
    m = nope_src.shape[0]
    tm = _pick(m, 256, 16)
    g_nope = g[:MLA_NOPE].reshape(1, MLA_NOPE).astype(F32)
    g_rope = jnp.pad(g[MLA_NOPE:], (0, LANES - MLA_ROPE)).reshape(1, LANES).astype(F32)
    nw = MLA_HEADS * MLA_NOPE
    nb = nope_off // nw
    if shared_rope:
        rope_spec = pl.BlockSpec((tm, LANES), lambda i: (i, 0))
    else:
        rw = MLA_HEADS * LANES
        rb = rope_off // rw
        rope_spec = pl.BlockSpec((tm, rw), lambda i: (i, rb))
    return pl.pallas_call(
        functools.partial(_mla_head_body, shared_rope=shared_rope, out_scale=out_scale),
        grid=(m // tm,),
        in_specs=[pl.BlockSpec((tm, nw), lambda i: (i, nb)), rope_spec,
                  pl.BlockSpec((1, MLA_NOPE), lambda i: (0, 0)),
                  pl.BlockSpec((1, LANES), lambda i: (0, 0)),
                  pl.BlockSpec((tm, LANES), lambda i: (i, 0)),
                  pl.BlockSpec((tm, LANES), lambda i: (i, 0))],
        out_specs=pl.BlockSpec((MLA_HEADS, tm, 2 * LANES), lambda i: (0, i, 0)),
        out_shape=jax.ShapeDtypeStruct((MLA_HEADS, m, 2 * LANES), BF16),
        compiler_params=_params("arbitrary"),
        name="mla_head_norm_rope",
    )(nope_src, rope_src, g_nope, g_rope, cos, sin)


FLASH_TQ = 1024
FLASH_TK = 2816
_NT_DIMS = (((1,), (1,)), ((), ()))
FLASH_ONES_ROWS = 16


def _flash_body(q_ref, k_ref, vt_ref, _, o_ref, *, n_chunks, tkc):
    q = q_ref[0]
    tq = q.shape[0]
    m = jnp.full((1, tq), NEG_INF, F32)
    acc = jnp.zeros((vt_ref.shape[2], tq), F32)
    for c in range(n_chunks):
        k_c = k_ref[0, c * tkc:(c + 1) * tkc, :]
        s = lax.dot_general(k_c, q, _NT_DIMS, preferred_element_type=F32)
        m_new = jnp.maximum(m, jnp.max(s, axis=0, keepdims=True))
        alpha = jnp.exp2(m - m_new)
        p = jnp.exp2(s - m_new).astype(BF16)
        acc = alpha * acc + jnp.dot(vt_ref[0, c], p, preferred_element_type=F32)
        m = m_new
    out = acc[:MLA_V] * (1.0 / acc[MLA_V:MLA_V + 1])
    o_ref[...] = out.T.astype(o_ref.dtype)


def _mla_attention(q, k, v_t, o_buf, o_off):
    h, s, dq = q.shape
    t = k.shape[1]
    n_chunks, dva, tkc = v_t.shape[1:]
    dv = dva - FLASH_ONES_ROWS
    tq = _pick(s, FLASH_TQ, LANES)
    ob = o_off // dv
    return pl.pallas_call(
        functools.partial(_flash_body, n_chunks=n_chunks, tkc=tkc),
        grid=(h, s // tq),
        in_specs=[pl.BlockSpec((1, tq, dq), lambda hh, i: (hh, i, 0)),
                  pl.BlockSpec((1, t, dq), lambda hh, i: (hh, 0, 0)),
                  pl.BlockSpec((1, n_chunks, dva, tkc), lambda hh, i: (hh, 0, 0, 0)),
                  pl.BlockSpec(memory_space=pl.ANY)],
        out_specs=pl.BlockSpec((tq, dv), lambda hh, i: (i, ob + hh)),
        out_shape=jax.ShapeDtypeStruct(o_buf.shape, o_buf.dtype),
        input_output_aliases={3: 0},
        compiler_params=_params("parallel", "arbitrary"),
        name="mla_flash_attention",
    )(q, k, v_t, o_buf)


def _na_variant_rows(rows, variant):
    r0 = (0, NA_Q_ROWS, rows - NA_Q_ROWS)[variant]
    ks = min(max(r0 - NA_WIN_H // 2, 0), rows - NA_K_ROWS)
    return r0, ks


def _na_bias_body(rpb_ref, o_ref, *, rows):
    h = pl.program_id(0)
    pair = 2 * GRID_W
    kc = lax.broadcasted_iota(jnp.int32, (GRID_W, pair), 0)
    lane = lax.broadcasted_iota(jnp.int32, (GRID_W, pair), 1)
    qc = jnp.where(lane < GRID_W, lane, lane - GRID_W)
    c_start = jnp.clip(qc - NA_WIN_W // 2, 0, GRID_W - NA_WIN_W)
    in_cols = (kc >= c_start) & (kc < c_start + NA_WIN_W)
    diag = kc - qc + (NA_WIN_W - 1)
    masked = jnp.full((GRID_W, pair), NEG_INF, F32)
    n_dr = 2 * NA_WIN_H - 1
    tiles = []
    for dr in range(n_dr):
        t = jnp.zeros((GRID_W, pair), F32)
        for jj in range(2 * NA_WIN_W - 1):
            t = jnp.where(diag == jj, rpb_ref[h, dr * (2 * NA_WIN_W - 1) + jj] * LOG2E, t)
        tiles.append(jnp.where(in_cols, t, masked))

    def row_tile(variant, a, b):
        r0, ks = _na_variant_rows(rows, variant)
        i, j = r0 + a, ks + b
        r_start = min(max(i - NA_WIN_H // 2, 0), rows - NA_WIN_H)
        if r_start <= j < r_start + NA_WIN_H:
            return tiles[j - i + NA_WIN_H - 1]
        return masked

    left = lane < GRID_W
    for variant in range(3):
        for b in range(NA_K_ROWS):
            for p in range(NA_Q_ROWS // 2):
                t0, t1 = row_tile(variant, 2 * p, b), row_tile(variant, 2 * p + 1, b)
                tile = t0 if t0 is t1 else jnp.where(left, t0, t1)
                o_ref[0, variant, b * GRID_W:(b + 1) * GRID_W, p * pair:(p + 1) * pair] = tile


def _na_bias_table(rpb, rows):
    h = rpb.shape[0]
    nq, nk = NA_Q_ROWS * GRID_W, NA_K_ROWS * GRID_W
    return pl.pallas_call(
        functools.partial(_na_bias_body, rows=rows),
        grid=(h,),
        in_specs=[pl.BlockSpec(memory_space=pltpu.SMEM)],
        out_specs=pl.BlockSpec((1, 3, nk, nq), lambda hh: (hh, 0, 0, 0)),
        out_shape=jax.ShapeDtypeStruct((h, 3, nk, nq), F32),
        compiler_params=_params("arbitrary"),
        name="na_bias_table",
    )(rpb.reshape(h, -1))


def _na_body(q_ref, k_ref, vt_ref, kc_ref, vct_ref, b_ref, _, o_ref, *, rows):
    rb = pl.program_id(1)
    ks = jnp.clip(rb * NA_Q_ROWS - NA_WIN_H // 2, 0, rows - NA_K_ROWS)
    start = pl.multiple_of(ks * GRID_W, NA_V_CHUNK)
    cs = lax.div(ks, NA_V_CHUNK // GRID_W)
    nk = NA_K_ROWS * GRID_W
    for g in range(NA_HEADS_PER_STEP):
        cols = slice(g * HEAD_DIM, (g + 1) * HEAD_DIM)
        q = q_ref[:, cols]
        s_win = lax.dot_general(k_ref[pl.ds(start, nk), cols], q, _NT_DIMS, preferred_element_type=F32)
        s_win = s_win + b_ref[g, 0]
        s_ctx = lax.dot_general(kc_ref[:, cols], q, _NT_DIMS, preferred_element_type=F32)
        m = jnp.maximum(jnp.max(s_win, axis=0, keepdims=True), jnp.max(s_ctx, axis=0, keepdims=True))
        acc = jnp.dot(vct_ref[g], jnp.exp2(s_ctx - m).astype(BF16), preferred_element_type=F32)
        p_win = jnp.exp2(s_win - m).astype(BF16)
        for i in range(nk // NA_V_CHUNK):
            acc = acc + jnp.dot(vt_ref[g, cs + i], p_win[i * NA_V_CHUNK:(i + 1) * NA_V_CHUNK],
                                preferred_element_type=F32)
        out = acc[:HEAD_DIM] * (1.0 / acc[HEAD_DIM:HEAD_DIM + 1])
        o_ref[:, cols] = out.T.astype(o_ref.dtype)


def _with_ones_rows(v_t):
    shape = v_t.shape[:-2] + (FLASH_ONES_ROWS, v_t.shape[-1])
    ones_rows = jnp.zeros(shape, v_t.dtype).at[..., 0, :].set(1.0)
    return jnp.concatenate([v_t, ones_rows], axis=-2)


def _na_attention(qk, v, ck, cv, bias, rows, o_buf):
    s = qk.shape[0]
    c = ck.shape[0]
    nq, nk = NA_Q_ROWS * GRID_W, NA_K_ROWS * GRID_W
    n_blk = rows // NA_Q_ROWS
    n_vc = s // NA_V_CHUNK
    assert nk % NA_V_CHUNK == 0 and ((rows - NA_K_ROWS) * GRID_W) % NA_V_CHUNK == 0
    v_t = _with_ones_rows(jnp.transpose(v.reshape(n_vc, NA_V_CHUNK, NA_HEADS, HEAD_DIM), (2, 0, 3, 1)))
    vc_t = _with_ones_rows(jnp.transpose(cv.reshape(c, NA_HEADS, HEAD_DIM), (1, 2, 0)))
    dva = HEAD_DIM + FLASH_ONES_ROWS

    def variant(rb):
        return jnp.where(rb == 0, 0, jnp.where(rb == n_blk - 1, 2, 1))

    hps = NA_HEADS_PER_STEP
    gw = hps * HEAD_DIM
    n_grp = NA_HEADS // hps
    return pl.pallas_call(
        functools.partial(_na_body, rows=rows),
        grid=(n_grp, n_blk),
        in_specs=[pl.BlockSpec((nq, gw), lambda hg, rb: (rb, hg)),
                  pl.BlockSpec((s, gw), lambda hg, rb: (0, n_grp + hg)),
                  pl.BlockSpec((hps, n_vc, dva, NA_V_CHUNK), lambda hg, rb: (hg, 0, 0, 0)),
                  pl.BlockSpec((c, gw), lambda hg, rb: (0, hg)),
                  pl.BlockSpec((hps, dva, c), lambda hg, rb: (hg, 0, 0)),
                  pl.BlockSpec((hps, 1, nk, nq), lambda hg, rb: (hg, variant(rb), 0, 0)),
                  pl.BlockSpec(memory_space=pl.ANY)],
        out_specs=pl.BlockSpec((nq, gw), lambda hg, rb: (rb, hg)),
        out_shape=jax.ShapeDtypeStruct(o_buf.shape, o_buf.dtype),
        input_output_aliases={6: 0},
        compiler_params=_params("parallel", "arbitrary"),
        name="neighbourhood_attention",
    )(qk, qk, v_t, ck, vc_t, bias, o_buf)


CONV_HALO = 16


SUBLANES = 8


def _dwconv_body(prev_ref, cur_ref, next_ref, w_ref, bdw_ref, g_ref, b_ref, o_ref, win_ref, conv_ref,
                 *, n_blk):
    i = pl.program_id(0)
    tm, d = cur_ref.shape
    nc = d // LANES
    has_prev = i > 0
    has_next = i < n_blk - 1
    for c in range(nc):
        sl = slice(c * LANES, (c + 1) * LANES)
        win_ref[c, 0:CONV_HALO, :] = jnp.where(has_prev, prev_ref[:, sl].astype(F32), 0.0)
        win_ref[c, CONV_HALO:CONV_HALO + tm, :] = cur_ref[:, sl].astype(F32)
        win_ref[c, CONV_HALO + tm:, :] = jnp.where(has_next, next_ref[:, sl].astype(F32), 0.0)

    groups = (CONV_WIDTH + SUBLANES) // SUBLANES
    span = tm + SUBLANES * (groups - 1)

    def chunk(c, carry):
        w = w_ref[c]
        acc = jnp.zeros((tm, LANES), F32) + bdw_ref[c]
        for b in range(SUBLANES):
            shifted = win_ref[c, pl.ds(b, span), :]
            for a in range(groups):
                o = SUBLANES * a + b
                if 1 <= o <= CONV_WIDTH:
                    acc = acc + shifted[SUBLANES * a:SUBLANES * a + tm, :] * w[o - 1:o, :]
        conv_ref[c] = acc
        return carry

    lax.fori_loop(0, nc, chunk, 0)

    total = conv_ref[0]
    for c in range(1, nc):
        total = total + conv_ref[c]
    mu = jnp.sum(total, axis=-1, keepdims=True) * (1.0 / d)
    sq = jnp.zeros((tm, LANES), F32)
    for c in range(nc):
        dlt = conv_ref[c] - mu
        sq = sq + dlt * dlt
    rstd = lax.rsqrt(jnp.sum(sq, axis=-1, keepdims=True) * (1.0 / d) + LN_EPS)
    for c in range(nc):
        y = (conv_ref[c] - mu) * rstd * g_ref[c] + b_ref[c]
        o_ref[:, c * LANES:(c + 1) * LANES] = (y * jax.nn.sigmoid(y)).astype(o_ref.dtype)


def _dwconv_ln_silu(u, w_dw, b_dw, g_ln, b_ln):
    m, d = u.shape
    tm = _pick(m, 128, CONV_HALO)
    nc = d // LANES
    n_blk = m // tm
    per = tm // CONV_HALO
    last_halo = m // CONV_HALO - 1
    assert CONV_WIDTH // 2 < CONV_HALO and CONV_WIDTH + 1 <= 2 * CONV_HALO
    vec = lambda v: v.reshape(nc, 1, LANES).astype(F32)
    w_chunks = jnp.transpose(w_dw.astype(F32).reshape(CONV_WIDTH, nc, LANES), (1, 0, 2))
    whole = lambda shape: pl.BlockSpec(shape, lambda i: (0,) * len(shape))
    return pl.pallas_call(
        functools.partial(_dwconv_body, n_blk=n_blk),
        grid=(n_blk,),
        in_specs=[pl.BlockSpec((CONV_HALO, d), lambda i: (jnp.maximum(i * per - 1, 0), 0)),
                  pl.BlockSpec((tm, d), lambda i: (i, 0)),
                  pl.BlockSpec((CONV_HALO, d), lambda i: (jnp.minimum((i + 1) * per, last_halo), 0)),
                  whole((nc, CONV_WIDTH, LANES)), whole((nc, 1, LANES)), whole((nc, 1, LANES)),
                  whole((nc, 1, LANES))],
        out_specs=pl.BlockSpec((tm, d), lambda i: (i, 0)),
        out_shape=jax.ShapeDtypeStruct((m, d), BF16),
        scratch_shapes=[pltpu.VMEM((nc, tm + 2 * CONV_HALO, LANES), F32),
                        pltpu.VMEM((nc, tm, LANES), F32)],
        compiler_params=_params("arbitrary"),
        name="dwconv_ln_silu",
    )(u, u, u, w_chunks, vec(b_dw), vec(g_ln), vec(b_ln))


def _rope_tables(n_ctx, n_tok):
    t = jnp.arange(n_tok, dtype=jnp.int32)
    pos = jnp.stack([t // GRID_W, t % GRID_W], axis=-1).astype(F32)
    n_freq = MLA_ROPE // 4
    inv_freq = ROPE_THETA ** (-jnp.arange(n_freq, dtype=F32) / n_freq)
    ang = (pos[:, :, None] * inv_freq).reshape(n_tok, 2 * n_freq)
    cos, sin = jnp.cos(ang), jnp.sin(ang)
    zeros = jnp.zeros((n_tok, LANES - MLA_ROPE), F32)
    cos_lat = jnp.concatenate([cos, cos, zeros], axis=-1)
    sin_lat = jnp.concatenate([-sin, sin, zeros], axis=-1)
    cos_ctx = jnp.concatenate([jnp.ones((n_ctx, MLA_ROPE), F32), jnp.zeros((n_ctx, LANES - MLA_ROPE), F32)], -1)
    sin_ctx = jnp.zeros((n_ctx, LANES), F32)
    return (jnp.concatenate([cos_ctx, cos_lat], 0), jnp.concatenate([sin_ctx, sin_lat], 0))


def _attention_layer(x, xc, mod, layer, j, p):
    s, d = x.shape
    c = xc.shape[0]
    rows = s // GRID_W
    assert s % GRID_W == 0 and rows % NA_Q_ROWS == 0 and rows >= NA_K_ROWS
    w_in = p["att_w_in"]

    hl = _rms_norm(x, p["g_mix"][layer], mod, layer, 0, 0, 1)
    hc = _rms_norm(xc, p["g_mix"][layer], mod, layer, 1, 0, 1)

    g_q = jnp.tile(p["att_g_na_q"][j], NA_HEADS) * (HEAD_DIM ** -0.5 * LOG2E)
    g_k = jnp.tile(p["att_g_na_k"][j], NA_HEADS)
    w_kr = jnp.pad(w_in[j, :, OFF_K_ROPE:], ((0, 0), (0, LANES - MLA_ROPE)))

    qk = _matmul(hl, w_in, w_layer=j, off=0, n=2 * NA_WIDTH, kind="headnorm",
                 gain=jnp.concatenate([g_q, g_k]))
    na_v = _matmul(hl, w_in, w_layer=j, off=OFF_NA_V, n=NA_WIDTH)
    q_lat = _matmul(hl, w_in, w_layer=j, off=OFF_Q_LAT, n=MLA_Q_RANK)
    kv_lat = _matmul(hl, w_in, w_layer=j, off=OFF_KV_LAT, n=MLA_KV_RANK)
    k_rope = _matmul(hl, w_kr, tn_pref=LANES)
    c_k = _matmul(hc, w_in, w_layer=j, off=OFF_NA_K, n=NA_WIDTH, kind="headnorm", gain=g_k)
    c_v = _matmul(hc, w_in, w_layer=j, off=OFF_NA_V, n=NA_WIDTH)
    c_kv_lat = _matmul(hc, w_in, w_layer=j, off=OFF_KV_LAT, n=MLA_KV_RANK)
    c_k_rope = _matmul(hc, w_kr, tn_pref=LANES)

    w_qb = p["att_w_qb"][j].reshape(MLA_Q_RANK, MLA_HEADS, MLA_QK)
    w_qb_rope = jnp.pad(w_qb[:, :, MLA_NOPE:], ((0, 0), (0, 0), (0, LANES - MLA_ROPE)))
    w_qb_perm = jnp.concatenate([w_qb[:, :, :MLA_NOPE].reshape(MLA_Q_RANK, -1),
                                 w_qb_rope.reshape(MLA_Q_RANK, -1)], axis=-1)
    w_kvb = p["att_w_kvb"][j].reshape(MLA_KV_RANK, MLA_HEADS, 2, MLA_NOPE)
    w_kvb_perm = jnp.transpose(w_kvb, (0, 2, 1, 3)).reshape(MLA_KV_RANK, -1)

    q_raw = _matmul(_rms_norm(q_lat, p["att_g_qa"][j]), w_qb_perm, tn_pref=1024)
    kv_n = _rms_norm(jnp.concatenate([c_kv_lat, kv_lat], axis=0), p["att_g_kva"][j])
    kv_raw = _matmul(kv_n, w_kvb_perm, tn_pref=2048)
    kr_all = jnp.concatenate([c_k_rope, k_rope], axis=0)

    cos, sin = _rope_tables(c, s)
    q_full = _mla_heads(q_raw, 0, q_raw, MLA_HEADS * MLA_NOPE, False, p["att_g_mla_q"][j],
                        cos[c:], sin[c:], MLA_QK ** -0.5 * LOG2E)
    k_full = _mla_heads(kv_raw, 0, kr_all, 0, True, p["att_g_mla_k"][j], cos, sin, 1.0)
    tkc = _pick(c + s, FLASH_TK, LANES)
    v_t = kv_raw[:, MLA_HEADS * MLA_NOPE:].reshape((c + s) // tkc, tkc, MLA_HEADS, MLA_V)
    v_t = _with_ones_rows(jnp.transpose(v_t, (2, 0, 3, 1)))

    bias = _na_bias_table(p["att_na_rpb"][j], rows)
    o = jnp.zeros((s, NA_WIDTH + MLA_HEADS * MLA_V), BF16)
    o = _na_attention(qk, na_v, c_k, c_v, bias, rows, o)
    o = _mla_attention(q_full, k_full, v_t, o, NA_WIDTH)
    return _matmul(o, p["att_w_out"], w_layer=j, kind="resid", out_dtype=F32, res=x, gate=mod,
                   gate_layer=layer, gate_row=0, gate_chunk=2)


def _conv_layer(x, mod, layer, j, p):
    d = x.shape[1]
    hl = _rms_norm(x, p["g_mix"][layer], mod, layer, 0, 0, 1)
    b1 = p["conv_b_pw1"][j]
    u = _matmul(hl, p["conv_w_pw1"], w_layer=j, off=0, off2=d, n=d, kind="glu", bias=b1[:d],
                bias2=b1[d:], tn_pref=256)
    v = _dwconv_ln_silu(u, p["conv_w_dw"][j], p["conv_b_dw"][j], p["conv_g_ln"][j], p["conv_b_ln"][j])
    return _matmul(v, p["conv_w_pw2"], w_layer=j, kind="resid", out_dtype=F32, bias=p["conv_b_pw2"][j],
                   res=x, gate=mod, gate_layer=layer, gate_row=0, gate_chunk=2)


def _mlp(x, mod, layer, p):
    h = _rms_norm(x, p["g_mlp"][layer], mod, layer, 0, 3, 4)
    u = _matmul(h, p["mlp_w1"], w_layer=layer, kind="relu2")
    return _matmul(u, p["mlp_w2"], w_layer=layer, kind="resid", out_dtype=F32, res=x, gate=mod,
                   gate_layer=layer, gate_row=0, gate_chunk=5, tm_pref=2048, tn_pref=1024, tk_pref=1024)


def kernel(x, c, ctx, c_ctx, ada_w, ada_b, g_mix, g_mlp, mlp_w1, mlp_w2, att_w_in, att_g_qa, att_w_qb, att_g_kva, att_w_kvb, att_g_na_q, att_g_na_k, att_na_rpb, att_g_mla_q, att_g_mla_k, att_w_out, conv_w_pw1, conv_b_pw1, conv_w_dw, conv_b_dw, conv_g_ln, conv_b_ln, conv_w_pw2, conv_b_pw2):
    p = dict(g_mix=g_mix, g_mlp=g_mlp, mlp_w1=mlp_w1, mlp_w2=mlp_w2, att_w_in=att_w_in,
             att_g_qa=att_g_qa, att_w_qb=att_w_qb, att_g_kva=att_g_kva, att_w_kvb=att_w_kvb,
             att_g_na_q=att_g_na_q, att_g_na_k=att_g_na_k, att_na_rpb=att_na_rpb,
             att_g_mla_q=att_g_mla_q, att_g_mla_k=att_g_mla_k, att_w_out=att_w_out,
             conv_w_pw1=conv_w_pw1, conv_b_pw1=conv_b_pw1, conv_w_dw=conv_w_dw, conv_b_dw=conv_b_dw,
             conv_g_ln=conv_g_ln, conv_b_ln=conv_b_ln, conv_w_pw2=conv_w_pw2, conv_b_pw2=conv_b_pw2)
    batch, _, d = x.shape
    depth = ada_w.shape[0]
    assert batch == 1 and c.shape[0] == 1
    svec = jnp.zeros((MOD_ROWS, d), F32).at[0].set(c[0]).at[1].set(c_ctx)
    mod = _modulation(svec, ada_w, ada_b)

    xl = x[0]
    xc = ctx[0]
    for layer in range(depth):
        j = layer // 2
        if layer % 2 == 0:
            assert not any(m % 2 == 0 for m in range(layer + 1, depth))
            xl = _attention_layer(xl, xc, mod, layer, j, p)
        else:
            xl = _conv_layer(xl, mod, layer, j, p)
        xl = _mlp(xl, mod, layer, p)
    return xl[None]
```

```python
import functools
import math

import jax
import jax.numpy as jnp
from jax import lax
from jax.experimental import pallas as pl
from jax.experimental.pallas import tpu as pltpu

F32 = jnp.float32
BF16 = jnp.bfloat16

GRID_W = 64
NA_HEADS = 16
HEAD_DIM = 128
NA_WIN_H = 8
NA_WIN_W = 16
NA_WIDTH = NA_HEADS * HEAD_DIM
MLA_HEADS = 16
MLA_Q_RANK = 1536
MLA_KV_RANK = 512
MLA_NOPE = 128
MLA_ROPE = 64
MLA_V = 128
MLA_QK = MLA_NOPE + MLA_ROPE
OFF_NA_K = NA_WIDTH
OFF_NA_V = 2 * NA_WIDTH
OFF_Q_LAT = 3 * NA_WIDTH
OFF_KV_LAT = OFF_Q_LAT + MLA_Q_RANK
OFF_K_ROPE = OFF_KV_LAT + MLA_KV_RANK
CONV_WIDTH = 31
ROPE_THETA = 10000.0
NORM_EPS = 1e-6
LN_EPS = 1e-5

LANES = 128
V7X_VMEM_LIMIT = 56 * 1024 * 1024
MOD_ROWS = 8
NA_Q_ROWS = 8
NA_K_ROWS = NA_Q_ROWS + NA_WIN_H
NA_V_CHUNK = 256
NA_HEADS_PER_STEP = 2
VT_ROWS_PER_STEP = 2048
NEG_INF = float("-inf")
LOG2E = math.log2(math.e)


def _pick(dim, pref, align):
    if dim <= pref:
        return dim
    for t in range(pref - pref % align, 0, -align):
        if dim % t == 0:
            return t
    raise ValueError(f"no tile for {dim}")


def _params(*sem):
    return pltpu.CompilerParams(dimension_semantics=sem, vmem_limit_bytes=V7X_VMEM_LIMIT)


def _mod_body(s_ref, w_ref, b_ref, o_ref):
    s = s_ref[...]
    s = s * jax.nn.sigmoid(s)
    o_ref[0] = jnp.dot(s.astype(BF16), w_ref[0].astype(BF16), preferred_element_type=F32) + b_ref[0]


def _modulation(svec, ada_w, ada_b):
    depth, d, n = ada_w.shape
    tn = _pick(n, 512, LANES)
    return pl.pallas_call(
        _mod_body,
        grid=(depth, n // tn),
        in_specs=[pl.BlockSpec((MOD_ROWS, d), lambda l, j: (0, 0)),
                  pl.BlockSpec((1, d, tn), lambda l, j: (l, 0, j)),
                  pl.BlockSpec((1, 1, tn), lambda l, j: (l, 0, j))],
        out_specs=pl.BlockSpec((1, MOD_ROWS, tn), lambda l, j: (l, 0, j)),
        out_shape=jax.ShapeDtypeStruct((depth, MOD_ROWS, n), F32),
        compiler_params=_params("arbitrary", "arbitrary"),
        name="modulation",
    )(svec, ada_w, ada_b.reshape(depth, 1, n))


def _norm_body(*refs, row, modulated):
    if modulated:
        x_ref, g_ref, shift_ref, scale_ref, o_ref = refs
    else:
        x_ref, g_ref, o_ref = refs
    x = x_ref[...].astype(F32)
    ms = jnp.mean(x * x, axis=-1, keepdims=True)
    y = x * lax.rsqrt(ms + NORM_EPS) * g_ref[...]
    if modulated:
        y = y * (1.0 + scale_ref[0, row:row + 1, :]) + shift_ref[0, row:row + 1, :]
    o_ref[...] = y.astype(o_ref.dtype)


def _rms_norm(x, g, mod=None, layer=0, row=0, shift_chunk=0, scale_chunk=1):
    m, d = x.shape
    tm = _pick(m, 512, 16)
    in_specs = [pl.BlockSpec((tm, d), lambda i: (i, 0)), pl.BlockSpec((1, d), lambda i: (0, 0))]
    args = [x, g.reshape(1, d)]
    if mod is not None:
        in_specs += [pl.BlockSpec((1, MOD_ROWS, d), lambda i: (layer, 0, shift_chunk)),
                     pl.BlockSpec((1, MOD_ROWS, d), lambda i: (layer, 0, scale_chunk))]
        args += [mod, mod]
    return pl.pallas_call(
        functools.partial(_norm_body, row=row, modulated=mod is not None),
        grid=(m // tm,),
        in_specs=in_specs,
        out_specs=pl.BlockSpec((tm, d), lambda i: (i, 0)),
        out_shape=jax.ShapeDtypeStruct((m, d), BF16),
        compiler_params=_params("arbitrary"),
        name="rms_norm",
    )(*args)


def _head_norm(acc, g):
    outs = []
    for c in range(acc.shape[1] // HEAD_DIM):
        blk = acc[:, c * HEAD_DIM:(c + 1) * HEAD_DIM]
        ms = jnp.mean(blk * blk, axis=-1, keepdims=True)
        outs.append(blk * lax.rsqrt(ms + NORM_EPS) * g[:, c * HEAD_DIM:(c + 1) * HEAD_DIM])
    return outs[0] if len(outs) == 1 else jnp.concatenate(outs, axis=-1)


def _epilogue(kind, acc, acc2, ex, gate_row):
    if kind == "plain":
        return acc
    if kind == "relu2":
        r = jnp.maximum(acc, 0.0)
        return r * r
    if kind == "headnorm":
        return _head_norm(acc, ex["gain"][...])
    if kind == "glu":
        b = ex["bias"][...]
        b2 = ex["bias2"][...]
        return (acc + b) * jax.nn.sigmoid(acc2 + b2)
    if kind == "resid":
        y = acc
        if "bias" in ex:
            y = y + ex["bias"][...]
        return ex["res"][...] + ex["gate"][0, gate_row:gate_row + 1, :] * y
    raise ValueError(kind)


def _mm_body(*refs, kind, names, dual, k_steps, gate_row):
    a_ref, w_ref = refs[0], refs[1]
    pos = 2
    w2_ref = None
    if dual:
        w2_ref = refs[pos]
        pos += 1
    ex = dict(zip(names, refs[pos:pos + len(names)]))
    pos += len(names)
    o_ref = refs[pos]
    scratch = refs[pos + 1:]

    a = a_ref[...]
    if k_steps == 1:
        part = jnp.dot(a, w_ref[...].astype(BF16), preferred_element_type=F32)
        part2 = jnp.dot(a, w2_ref[...].astype(BF16), preferred_element_type=F32) if dual else None
        o_ref[...] = _epilogue(kind, part, part2, ex, gate_row).astype(o_ref.dtype)
        return

    acc_ref = scratch[0] if scratch else o_ref
    k = pl.program_id(2)

    @pl.when(k == 0)
    def _():
        acc_ref[...] = jnp.zeros(acc_ref.shape, F32)

    acc_ref[...] += jnp.dot(a, w_ref[...].astype(BF16), preferred_element_type=F32)

    @pl.when(k == k_steps - 1)
    def _():
        o_ref[...] = _epilogue(kind, acc_ref[...], None, ex, gate_row).astype(o_ref.dtype)


def _matmul(a, w, *, off=0, n=None, kind="plain", out_dtype=BF16, off2=None,
            gain=None, bias=None, bias2=None, res=None, gate=None, gate_layer=0, gate_row=0,
            gate_chunk=0, w_layer=None, tm_pref=1024, tn_pref=512, tk_pref=4096):
    m, kdim = a.shape
    n = w.shape[-1] if n is None else n
    tm = _pick(m, tm_pref, 16)
    tn = _pick(n, tn_pref, LANES)
    tk = _pick(kdim, tk_pref, LANES)
    assert off % tn == 0 and (off2 is None or off2 % tn == 0)
    k_steps = kdim // tk
    dual = off2 is not None
    assert not (dual and k_steps > 1)
    ob, ob2 = off // tn, (off2 // tn if dual else 0)

    def w_spec(col0):
        if w_layer is None:
            return pl.BlockSpec((tk, tn), lambda i, j, k: (k, col0 + j))
        return pl.BlockSpec((None, tk, tn), lambda i, j, k: (w_layer, k, col0 + j))

    in_specs = [pl.BlockSpec((tm, tk), lambda i, j, k: (i, k)), w_spec(ob)]
    args = [a, w]
    if dual:
        in_specs.append(w_spec(ob2))
        args.append(w)
    names = []
    for name, vec in (("gain", gain), ("bias", bias), ("bias2", bias2)):
        if vec is not None:
            names.append(name)
            in_specs.append(pl.BlockSpec((1, tn), lambda i, j, k: (0, j)))
            args.append(vec.reshape(1, n).astype(F32))
    if res is not None:
        names.append("res")
        in_specs.append(pl.BlockSpec((tm, tn), lambda i, j, k: (i, j)))
        args.append(res)
    if gate is not None:
        names.append("gate")
        gb = gate_chunk * (n // tn)
        in_specs.append(pl.BlockSpec((1, MOD_ROWS, tn), lambda i, j, k: (gate_layer, 0, gb + j)))
        args.append(gate)

    scratch = [pltpu.VMEM((tm, tn), F32)] if k_steps > 1 and out_dtype != F32 else []
    return pl.pallas_call(
        functools.partial(_mm_body, kind=kind, names=tuple(names), dual=dual, k_steps=k_steps,
                          gate_row=gate_row),
        grid=(m // tm, n // tn, k_steps),
        in_specs=in_specs,
        out_specs=pl.BlockSpec((tm, tn), lambda i, j, k: (i, j)),
        out_shape=jax.ShapeDtypeStruct((m, n), out_dtype),
        scratch_shapes=scratch,
        compiler_params=_params("parallel", "parallel", "arbitrary"),
        name="matmul_" + kind,
    )(*args)


def _mla_head_body(nope_ref, rope_ref, gn_ref, gr_ref, cos_ref, sin_ref, o_ref, *, shared_rope,
                   out_scale):
    lane = lax.broadcasted_iota(jnp.int32, (1, LANES), 1)
    first_half = lane < MLA_ROPE // 2
    cos = cos_ref[...]
    sin = sin_ref[...]
    gn = gn_ref[...]
    gr = gr_ref[...]
    for h in range(MLA_HEADS):
        nope = nope_ref[:, h * MLA_NOPE:(h + 1) * MLA_NOPE].astype(F32)
        if shared_rope:
            rope = rope_ref[...].astype(F32)
        else:
            rope = rope_ref[:, h * LANES:(h + 1) * LANES].astype(F32)
        ss = jnp.sum(nope * nope, axis=-1, keepdims=True) + jnp.sum(rope * rope, axis=-1, keepdims=True)
        inv = lax.rsqrt(ss * (1.0 / MLA_QK) + NORM_EPS)
        nope = nope * inv * gn
        rope = rope * inv * gr
        partner = jnp.where(first_half, pltpu.roll(rope, LANES - MLA_ROPE // 2, 1),
                            pltpu.roll(rope, MLA_ROPE // 2, 1))
        rope = rope * cos + partner * sin
        o_ref[h, :, 0:MLA_NOPE] = (nope * out_scale).astype(o_ref.dtype)
        o_ref[h, :, MLA_NOPE:] = (rope * out_scale).astype(o_ref.dtype)


def _mla_heads(nope_src, nope_off, rope_src, rope_off, shared_rope, g, cos, sin, out_scale):
    m = nope_src.shape[0]
    tm = _pick(m, 256, 16)
    g_nope = g[:MLA_NOPE].reshape(1, MLA_NOPE).astype(F32)
    g_rope = jnp.pad(g[MLA_NOPE:], (0, LANES - MLA_ROPE)).reshape(1, LANES).astype(F32)
    nw = MLA_HEADS * MLA_NOPE
    nb = nope_off // nw
    if shared_rope:
        rope_spec = pl.BlockSpec((tm, LANES), lambda i: (i, 0))
    else:
        rw = MLA_HEADS * LANES
        rb = rope_off // rw
        rope_spec = pl.BlockSpec((tm, rw), lambda i: (i, rb))
    return pl.pallas_call(
        functools.partial(_mla_head_body, shared_rope=shared_rope, out_scale=out_scale),
        grid=(m // tm,),
        in_specs=[pl.BlockSpec((tm, nw), lambda i: (i, nb)), rope_spec,
                  pl.BlockSpec((1, MLA_NOPE), lambda i: (0, 0)),
                  pl.BlockSpec((1, LANES), lambda i: (0, 0)),
                  pl.BlockSpec((tm, LANES), lambda i: (i, 0)),
                  pl.BlockSpec((tm, LANES), lambda i: (i, 0))],
        out_specs=pl.BlockSpec((MLA_HEADS, tm, 2 * LANES), lambda i: (0, i, 0)),
        out_shape=jax.ShapeDtypeStruct((MLA_HEADS, m, 2 * LANES), BF16),
        compiler_params=_params("arbitrary"),
        name="mla_head_norm_rope",
    )(nope_src, rope_src, g_nope, g_rope, cos, sin)


FLASH_TQ = 1024
FLASH_TK = 2816
_NT_DIMS = (((1,), (1,)), ((), ()))
FLASH_ONES_ROWS = 16
SOFTMAX_SAFE_BOUND = 60.0
BF16_NORM_SLACK = 1.01


def _flash_body(bound_ref, q_ref, k_ref, vt_ref, _, o_ref, *, n_chunks, tkc):
    q = q_ref[0]
    tq = q.shape[0]
    bound = bound_ref[0]

    def scores(c):
        k_c = k_ref[0, c * tkc:(c + 1) * tkc, :]
        return lax.dot_general(k_c, q, _NT_DIMS, preferred_element_type=F32)

    def finish(acc):
        out = acc[:MLA_V] * (1.0 / acc[MLA_V:MLA_V + 1])
        o_ref[...] = out.T.astype(o_ref.dtype)

    @pl.when(bound <= SOFTMAX_SAFE_BOUND)
    def _():
        acc = jnp.zeros((vt_ref.shape[2], tq), F32)
        for c in range(n_chunks):
            p = jnp.exp2(scores(c) - bound).astype(BF16)
            acc = acc + jnp.dot(vt_ref[0, c], p, preferred_element_type=F32)
        finish(acc)

    @pl.when(bound > SOFTMAX_SAFE_BOUND)
    def _():
        m = jnp.full((1, tq), NEG_INF, F32)
        acc = jnp.zeros((vt_ref.shape[2], tq), F32)
        for c in range(n_chunks):
            s = scores(c)
            m_new = jnp.maximum(m, jnp.max(s, axis=0, keepdims=True))
            alpha = jnp.exp2(m - m_new)
            p = jnp.exp2(s - m_new).astype(BF16)
            acc = alpha * acc + jnp.dot(vt_ref[0, c], p, preferred_element_type=F32)
            m = m_new
        finish(acc)


def _mla_attention(q, k, v_t, o_buf, o_off, bound):
    h, s, dq = q.shape
    t = k.shape[1]
    n_chunks, dva, tkc = v_t.shape[1:]
    dv = dva - FLASH_ONES_ROWS
    tq = _pick(s, FLASH_TQ, LANES)
    ob = o_off // dv
    return pl.pallas_call(
        functools.partial(_flash_body, n_chunks=n_chunks, tkc=tkc),
        grid=(h, s // tq),
        in_specs=[pl.BlockSpec(memory_space=pltpu.SMEM),
                  pl.BlockSpec((1, tq, dq), lambda hh, i: (hh, i, 0)),
                  pl.BlockSpec((1, t, dq), lambda hh, i: (hh, 0, 0)),
                  pl.BlockSpec((1, n_chunks, dva, tkc), lambda hh, i: (hh, 0, 0, 0)),
                  pl.BlockSpec(memory_space=pl.ANY)],
        out_specs=pl.BlockSpec((tq, dv), lambda hh, i: (i, ob + hh)),
        out_shape=jax.ShapeDtypeStruct(o_buf.shape, o_buf.dtype),
        input_output_aliases={4: 0},
        compiler_params=_params("parallel", "arbitrary"),
        name="mla_flash_attention",
    )(bound, q, k, v_t, o_buf)


def _na_variant_rows(rows, variant):
    r0 = (0, NA_Q_ROWS, rows - NA_Q_ROWS)[variant]
    ks = min(max(r0 - NA_WIN_H // 2, 0), rows - NA_K_ROWS)
    return r0, ks


def _na_bias_body(rpb_ref, o_ref, *, rows):
    h = pl.program_id(0)
    pair = 2 * GRID_W
    kc = lax.broadcasted_iota(jnp.int32, (GRID_W, pair), 0)
    lane = lax.broadcasted_iota(jnp.int32, (GRID_W, pair), 1)
    qc = jnp.where(lane < GRID_W, lane, lane - GRID_W)
    c_start = jnp.clip(qc - NA_WIN_W // 2, 0, GRID_W - NA_WIN_W)
    in_cols = (kc >= c_start) & (kc < c_start + NA_WIN_W)
    diag = kc - qc + (NA_WIN_W - 1)
    masked = jnp.full((GRID_W, pair), NEG_INF, F32)
    n_dr = 2 * NA_WIN_H - 1
    tiles = []
    for dr in range(n_dr):
        t = jnp.zeros((GRID_W, pair), F32)
        for jj in range(2 * NA_WIN_W - 1):
            t = jnp.where(diag == jj, rpb_ref[h, dr * (2 * NA_WIN_W - 1) + jj] * LOG2E, t)
        tiles.append(jnp.where(in_cols, t, masked))

    def row_tile(variant, a, b):
        r0, ks = _na_variant_rows(rows, variant)
        i, j = r0 + a, ks + b
        r_start = min(max(i - NA_WIN_H // 2, 0), rows - NA_WIN_H)
        if r_start <= j < r_start + NA_WIN_H:
            return tiles[j - i + NA_WIN_H - 1]
        return masked

    left = lane < GRID_W
    for variant in range(3):
        for b in range(NA_K_ROWS):
            for p in range(NA_Q_ROWS // 2):
                t0, t1 = row_tile(variant, 2 * p, b), row_tile(variant, 2 * p + 1, b)
                tile = t0 if t0 is t1 else jnp.where(left, t0, t1)
                o_ref[0, variant, b * GRID_W:(b + 1) * GRID_W, p * pair:(p + 1) * pair] = tile


def _na_bias_table(rpb, rows):
    h = rpb.shape[0]
    nq, nk = NA_Q_ROWS * GRID_W, NA_K_ROWS * GRID_W
    return pl.pallas_call(
        functools.partial(_na_bias_body, rows=rows),
        grid=(h,),
        in_specs=[pl.BlockSpec(memory_space=pltpu.SMEM)],
        out_specs=pl.BlockSpec((1, 3, nk, nq), lambda hh: (hh, 0, 0, 0)),
        out_shape=jax.ShapeDtypeStruct((h, 3, nk, nq), F32),
        compiler_params=_params("arbitrary"),
        name="na_bias_table",
    )(rpb.reshape(h, -1))


def _na_body(q_ref, k_ref, vt_ref, kc_ref, vct_ref, b_ref, _, o_ref, *, rows):
    rb = pl.program_id(1)
    ks = jnp.clip(rb * NA_Q_ROWS - NA_WIN_H // 2, 0, rows - NA_K_ROWS)
    start = pl.multiple_of(ks * GRID_W, NA_V_CHUNK)
    cs = lax.div(ks, NA_V_CHUNK // GRID_W)
    nk = NA_K_ROWS * GRID_W
    for g in range(NA_HEADS_PER_STEP):
        cols = slice(g * HEAD_DIM, (g + 1) * HEAD_DIM)
        q = q_ref[:, cols]
        s_win = lax.dot_general(k_ref[pl.ds(start, nk), cols], q, _NT_DIMS, preferred_element_type=F32)
        s_win = s_win + b_ref[g, 0]
        s_ctx = lax.dot_general(kc_ref[:, cols], q, _NT_DIMS, preferred_element_type=F32)
        m = jnp.maximum(jnp.max(s_win, axis=0, keepdims=True), jnp.max(s_ctx, axis=0, keepdims=True))
        acc = jnp.dot(vct_ref[g], jnp.exp2(s_ctx - m).astype(BF16), preferred_element_type=F32)
        p_win = jnp.exp2(s_win - m).astype(BF16)
        for i in range(nk // NA_V_CHUNK):
            acc = acc + jnp.dot(vt_ref[g, cs + i], p_win[i * NA_V_CHUNK:(i + 1) * NA_V_CHUNK],
                                preferred_element_type=F32)
        out = acc[:HEAD_DIM] * (1.0 / acc[HEAD_DIM:HEAD_DIM + 1])
        o_ref[:, cols] = out.T.astype(o_ref.dtype)


def _vt_body(v_ref, o_ref, *, chunk, per_step):
    dv = v_ref.shape[1]
    row = lax.broadcasted_iota(jnp.int32, (FLASH_ONES_ROWS, chunk), 0)
    ones_rows = jnp.where(row == 0, 1.0, 0.0).astype(o_ref.dtype)
    for i in range(per_step):
        x = v_ref[i * chunk:(i + 1) * chunk, :].astype(F32)
        o_ref[0, i, 0:dv, :] = x.T.astype(o_ref.dtype)
        o_ref[0, i, dv:, :] = ones_rows


def _values_transposed(src, off, n_heads, dv, chunk):
    t = src.shape[0]
    n_chunks = t // chunk
    per_step = _pick(n_chunks, max(1, VT_ROWS_PER_STEP // chunk), 1)
    cb = off // dv
    return pl.pallas_call(
        functools.partial(_vt_body, chunk=chunk, per_step=per_step),
        grid=(n_heads, n_chunks // per_step),
        in_specs=[pl.BlockSpec((per_step * chunk, dv), lambda hh, r: (r, cb + hh))],
        out_specs=pl.BlockSpec((1, per_step, dv + FLASH_ONES_ROWS, chunk), lambda hh, r: (hh, r, 0, 0)),
        out_shape=jax.ShapeDtypeStruct((n_heads, n_chunks, dv + FLASH_ONES_ROWS, chunk), src.dtype),
        compiler_params=_params("arbitrary", "arbitrary"),
        name="values_transposed",
    )(src)


def _na_attention(qk, v, ck, cv, bias, rows, o_buf):
    s = qk.shape[0]
    c = ck.shape[0]
    nq, nk = NA_Q_ROWS * GRID_W, NA_K_ROWS * GRID_W
    n_blk = rows // NA_Q_ROWS
    n_vc = s // NA_V_CHUNK
    assert nk % NA_V_CHUNK == 0 and ((rows - NA_K_ROWS) * GRID_W) % NA_V_CHUNK == 0
    v_t = _values_transposed(v, 0, NA_HEADS, HEAD_DIM, NA_V_CHUNK)
    vc_t = _values_transposed(cv, 0, NA_HEADS, HEAD_DIM, c)[:, 0]
    dva = HEAD_DIM + FLASH_ONES_ROWS

    def variant(rb):
        return jnp.where(rb == 0, 0, jnp.where(rb == n_blk - 1, 2, 1))

    hps = NA_HEADS_PER_STEP
    gw = hps * HEAD_DIM
    n_grp = NA_HEADS // hps
    return pl.pallas_call(
        functools.partial(_na_body, rows=rows),
        grid=(n_grp, n_blk),
        in_specs=[pl.BlockSpec((nq, gw), lambda hg, rb: (rb, hg)),
                  pl.BlockSpec((s, gw), lambda hg, rb: (0, n_grp + hg)),
                  pl.BlockSpec((hps, n_vc, dva, NA_V_CHUNK), lambda hg, rb: (hg, 0, 0, 0)),
                  pl.BlockSpec((c, gw), lambda hg, rb: (0, hg)),
                  pl.BlockSpec((hps, dva, c), lambda hg, rb: (hg, 0, 0)),
                  pl.BlockSpec((hps, 1, nk, nq), lambda hg, rb: (hg, variant(rb), 0, 0)),
                  pl.BlockSpec(memory_space=pl.ANY)],
        out_specs=pl.BlockSpec((nq, gw), lambda hg, rb: (rb, hg)),
        out_shape=jax.ShapeDtypeStruct(o_buf.shape, o_buf.dtype),
        input_output_aliases={6: 0},
        compiler_params=_params("parallel", "arbitrary"),
        name="neighbourhood_attention",
    )(qk, qk, v_t, ck, vc_t, bias, o_buf)


CONV_HALO = 16


SUBLANES = 8


def _dwconv_body(prev_ref, cur_ref, next_ref, w_ref, bdw_ref, g_ref, b_ref, o_ref, win_ref, conv_ref,
                 *, n_blk):
    i = pl.program_id(0)
    tm, d = cur_ref.shape
    nc = d // LANES
    has_prev = i > 0
    has_next = i < n_blk - 1
    for c in range(nc):
        sl = slice(c * LANES, (c + 1) * LANES)
        win_ref[c, 0:CONV_HALO, :] = jnp.where(has_prev, prev_ref[:, sl].astype(F32), 0.0)
        win_ref[c, CONV_HALO:CONV_HALO + tm, :] = cur_ref[:, sl].astype(F32)
        win_ref[c, CONV_HALO + tm:, :] = jnp.where(has_next, next_ref[:, sl].astype(F32), 0.0)

    groups = (CONV_WIDTH + SUBLANES) // SUBLANES
    span = tm + SUBLANES * (groups - 1)

    def chunk(c, carry):
        w = w_ref[c]
        acc = jnp.zeros((tm, LANES), F32) + bdw_ref[c]
        for b in range(SUBLANES):
            shifted = win_ref[c, pl.ds(b, span), :]
            for a in range(groups):
                o = SUBLANES * a + b
                if 1 <= o <= CONV_WIDTH:
                    acc = acc + shifted[SUBLANES * a:SUBLANES * a + tm, :] * w[o - 1:o, :]
        conv_ref[c] = acc
        return carry

    lax.fori_loop(0, nc, chunk, 0)

    total = conv_ref[0]
    for c in range(1, nc):
        total = total + conv_ref[c]
    mu = jnp.sum(total, axis=-1, keepdims=True) * (1.0 / d)
    sq = jnp.zeros((tm, LANES), F32)
    for c in range(nc):
        dlt = conv_ref[c] - mu
        sq = sq + dlt * dlt
    rstd = lax.rsqrt(jnp.sum(sq, axis=-1, keepdims=True) * (1.0 / d) + LN_EPS)
    for c in range(nc):
        y = (conv_ref[c] - mu) * rstd * g_ref[c] + b_ref[c]
        o_ref[:, c * LANES:(c + 1) * LANES] = (y * jax.nn.sigmoid(y)).astype(o_ref.dtype)


def _dwconv_ln_silu(u, w_dw, b_dw, g_ln, b_ln):
    m, d = u.shape
    tm = _pick(m, 128, CONV_HALO)
    nc = d // LANES
    n_blk = m // tm
    per = tm // CONV_HALO
    last_halo = m // CONV_HALO - 1
    assert CONV_WIDTH // 2 < CONV_HALO and CONV_WIDTH + 1 <= 2 * CONV_HALO
    vec = lambda v: v.reshape(nc, 1, LANES).astype(F32)
    w_chunks = jnp.transpose(w_dw.astype(F32).reshape(CONV_WIDTH, nc, LANES), (1, 0, 2))
    whole = lambda shape: pl.BlockSpec(shape, lambda i: (0,) * len(shape))
    return pl.pallas_call(
        functools.partial(_dwconv_body, n_blk=n_blk),
        grid=(n_blk,),
        in_specs=[pl.BlockSpec((CONV_HALO, d), lambda i: (jnp.maximum(i * per - 1, 0), 0)),
                  pl.BlockSpec((tm, d), lambda i: (i, 0)),
                  pl.BlockSpec((CONV_HALO, d), lambda i: (jnp.minimum((i + 1) * per, last_halo), 0)),
                  whole((nc, CONV_WIDTH, LANES)), whole((nc, 1, LANES)), whole((nc, 1, LANES)),
                  whole((nc, 1, LANES))],
        out_specs=pl.BlockSpec((tm, d), lambda i: (i, 0)),
        out_shape=jax.ShapeDtypeStruct((m, d), BF16),
        scratch_shapes=[pltpu.VMEM((nc, tm + 2 * CONV_HALO, LANES), F32),
                        pltpu.VMEM((nc, tm, LANES), F32)],
        compiler_params=_params("arbitrary"),
        name="dwconv_ln_silu",
    )(u, u, u, w_chunks, vec(b_dw), vec(g_ln), vec(b_ln))


def _rope_tables(n_ctx, n_tok):
    t = jnp.arange(n_tok, dtype=jnp.int32)
    pos = jnp.stack([t // GRID_W, t % GRID_W], axis=-1).astype(F32)
    n_freq = MLA_ROPE // 4
    inv_freq = ROPE_THETA ** (-jnp.arange(n_freq, dtype=F32) / n_freq)
    ang = (pos[:, :, None] * inv_freq).reshape(n_tok, 2 * n_freq)
    cos, sin = jnp.cos(ang), jnp.sin(ang)
    zeros = jnp.zeros((n_tok, LANES - MLA_ROPE), F32)
    cos_lat = jnp.concatenate([cos, cos, zeros], axis=-1)
    sin_lat = jnp.concatenate([-sin, sin, zeros], axis=-1)
    cos_ctx = jnp.concatenate([jnp.ones((n_ctx, MLA_ROPE), F32), jnp.zeros((n_ctx, LANES - MLA_ROPE), F32)], -1)
    sin_ctx = jnp.zeros((n_ctx, LANES), F32)
    return (jnp.concatenate([cos_ctx, cos_lat], 0), jnp.concatenate([sin_ctx, sin_lat], 0))


def _attention_layer(x, xc, mod, layer, j, p):
    s, d = x.shape
    c = xc.shape[0]
    rows = s // GRID_W
    assert s % GRID_W == 0 and rows % NA_Q_ROWS == 0 and rows >= NA_K_ROWS
    w_in = p["att_w_in"]

    hl = _rms_norm(x, p["g_mix"][layer], mod, layer, 0, 0, 1)
    hc = _rms_norm(xc, p["g_mix"][layer], mod, layer, 1, 0, 1)

    g_q = jnp.tile(p["att_g_na_q"][j], NA_HEADS) * (HEAD_DIM ** -0.5 * LOG2E)
    g_k = jnp.tile(p["att_g_na_k"][j], NA_HEADS)
    w_kr = jnp.pad(w_in[j, :, OFF_K_ROPE:], ((0, 0), (0, LANES - MLA_ROPE)))

    qk = _matmul(hl, w_in, w_layer=j, off=0, n=2 * NA_WIDTH, kind="headnorm",
                 gain=jnp.concatenate([g_q, g_k]))
    na_v = _matmul(hl, w_in, w_layer=j, off=OFF_NA_V, n=NA_WIDTH)
    q_lat = _matmul(hl, w_in, w_layer=j, off=OFF_Q_LAT, n=MLA_Q_RANK)
    kv_lat = _matmul(hl, w_in, w_layer=j, off=OFF_KV_LAT, n=MLA_KV_RANK)
    k_rope = _matmul(hl, w_kr, tn_pref=LANES)
    c_k = _matmul(hc, w_in, w_layer=j, off=OFF_NA_K, n=NA_WIDTH, kind="headnorm", gain=g_k)
    c_v = _matmul(hc, w_in, w_layer=j, off=OFF_NA_V, n=NA_WIDTH)
    c_kv_lat = _matmul(hc, w_in, w_layer=j, off=OFF_KV_LAT, n=MLA_KV_RANK)
    c_k_rope = _matmul(hc, w_kr, tn_pref=LANES)

    w_qb = p["att_w_qb"][j].reshape(MLA_Q_RANK, MLA_HEADS, MLA_QK)
    w_qb_rope = jnp.pad(w_qb[:, :, MLA_NOPE:], ((0, 0), (0, 0), (0, LANES - MLA_ROPE)))
    w_qb_perm = jnp.concatenate([w_qb[:, :, :MLA_NOPE].reshape(MLA_Q_RANK, -1),
                                 w_qb_rope.reshape(MLA_Q_RANK, -1)], axis=-1)
    w_kvb = p["att_w_kvb"][j].reshape(MLA_KV_RANK, MLA_HEADS, 2, MLA_NOPE)
    w_kvb_perm = jnp.transpose(w_kvb, (0, 2, 1, 3)).reshape(MLA_KV_RANK, -1)

    q_raw = _matmul(_rms_norm(q_lat, p["att_g_qa"][j]), w_qb_perm, tn_pref=1024)
    kv_n = _rms_norm(jnp.concatenate([c_kv_lat, kv_lat], axis=0), p["att_g_kva"][j])
    kv_raw = _matmul(kv_n, w_kvb_perm, tn_pref=4096)
    kr_all = jnp.concatenate([c_k_rope, k_rope], axis=0)

    cos, sin = _rope_tables(c, s)
    q_full = _mla_heads(q_raw, 0, q_raw, MLA_HEADS * MLA_NOPE, False, p["att_g_mla_q"][j],
                        cos[c:], sin[c:], MLA_QK ** -0.5 * LOG2E)
    k_full = _mla_heads(kv_raw, 0, kr_all, 0, True, p["att_g_mla_k"][j], cos, sin, 1.0)
    tkc = _pick(c + s, FLASH_TK, LANES)
    v_t = _values_transposed(kv_raw, MLA_HEADS * MLA_NOPE, MLA_HEADS, MLA_V, tkc)

    bias = _na_bias_table(p["att_na_rpb"][j], rows)
    o = jnp.zeros((s, NA_WIDTH + MLA_HEADS * MLA_V), BF16)
    o = _na_attention(qk, na_v, c_k, c_v, bias, rows, o)
    bound = (MLA_QK * MLA_QK ** -0.5 * LOG2E * BF16_NORM_SLACK) * (
        jnp.max(jnp.abs(p["att_g_mla_q"][j])) * jnp.max(jnp.abs(p["att_g_mla_k"][j])))
    o = _mla_attention(q_full, k_full, v_t, o, NA_WIDTH, bound.reshape(1).astype(F32))
    return _matmul(o, p["att_w_out"], w_layer=j, kind="resid", out_dtype=F32, res=x, gate=mod,
                   gate_layer=layer, gate_row=0, gate_chunk=2)


def _conv_layer(x, mod, layer, j, p):
    d = x.shape[1]
    hl = _rms_norm(x, p["g_mix"][layer], mod, layer, 0, 0, 1)
    b1 = p["conv_b_pw1"][j]
    u = _matmul(hl, p["conv_w_pw1"], w_layer=j, off=0, off2=d, n=d, kind="glu", bias=b1[:d],
                bias2=b1[d:], tn_pref=256)
    v = _dwconv_ln_silu(u, p["conv_w_dw"][j], p["conv_b_dw"][j], p["conv_g_ln"][j], p["conv_b_ln"][j])
    return _matmul(v, p["conv_w_pw2"], w_layer=j, kind="resid", out_dtype=F32, bias=p["conv_b_pw2"][j],
                   res=x, gate=mod, gate_layer=layer, gate_row=0, gate_chunk=2)


def _mlp(x, mod, layer, p):
    h = _rms_norm(x, p["g_mlp"][layer], mod, layer, 0, 3, 4)
    u = _matmul(h, p["mlp_w1"], w_layer=layer, kind="relu2")
    return _matmul(u, p["mlp_w2"], w_layer=layer, kind="resid", out_dtype=F32, res=x, gate=mod,
                   gate_layer=layer, gate_row=0, gate_chunk=5, tm_pref=2048, tn_pref=1024, tk_pref=1024)


def kernel(x, c, ctx, c_ctx, ada_w, ada_b, g_mix, g_mlp, mlp_w1, mlp_w2, att_w_in, att_g_qa, att_w_qb, att_g_kva, att_w_kvb, att_g_na_q, att_g_na_k, att_na_rpb, att_g_mla_q, att_g_mla_k, att_w_out, conv_w_pw1, conv_b_pw1, conv_w_dw, conv_b_dw, conv_g_ln, conv_b_ln, conv_w_pw2, conv_b_pw2):
    p = dict(g_mix=g_mix, g_mlp=g_mlp, mlp_w1=mlp_w1, mlp_w2=mlp_w2, att_w_in=att_w_in,
             att_g_qa=att_g_qa, att_w_qb=att_w_qb, att_g_kva=att_g_kva, att_w_kvb=att_w_kvb,
             att_g_na_q=att_g_na_q, att_g_na_k=att_g_na_k, att_na_rpb=att_na_rpb,
             att_g_mla_q=att_g_mla_q, att_g_mla_k=att_g_mla_k, att_w_out=att_w_out,
             conv_w_pw1=conv_w_pw1, conv_b_pw1=conv_b_pw1, conv_w_dw=conv_w_dw, conv_b_dw=conv_b_dw,
             conv_g_ln=conv_g_ln, conv_b_ln=conv_b_ln, conv_w_pw2=conv_w_pw2, conv_b_pw2=conv_b_pw2)
    batch, _, d = x.shape
    depth = ada_w.shape[0]
    assert batch == 1 and c.shape[0] == 1
    svec = jnp.zeros((MOD_ROWS, d), F32).at[0].set(c[0]).at[1].set(c_ctx)
    mod = _modulation(svec, ada_w, ada_b)

    xl = x[0]
    xc = ctx[0]
    for layer in range(depth):
        j = layer // 2
        if layer % 2 == 0:
            assert not any(m % 2 == 0 for m in range(layer + 1, depth))
            xl = _attention_layer(xl, xc, mod, layer, j, p)
        else:
            xl = _conv_layer(xl, mod, layer, j, p)
        xl = _mlp(xl, mod, layer, p)
    return xl[None]
```

```python
import functools
import math

import jax
import jax.numpy as jnp
from jax import lax
from jax.experimental import pallas as pl
from jax.experimental.pallas import tpu as pltpu

F32 = jnp.float32
BF16 = jnp.bfloat16

GRID_W = 64
NA_HEADS = 16
HEAD_DIM = 128
NA_WIN_H = 8
NA_WIN_W = 16
NA_WIDTH = NA_HEADS * HEAD_DIM
MLA_HEADS = 16
MLA_Q_RANK = 1536
MLA_KV_RANK = 512
MLA_NOPE = 128
MLA_ROPE = 64
MLA_V = 128
MLA_QK = MLA_NOPE + MLA_ROPE
OFF_NA_K = NA_WIDTH
OFF_NA_V = 2 * NA_WIDTH
OFF_Q_LAT = 3 * NA_WIDTH
OFF_KV_LAT = OFF_Q_LAT + MLA_Q_RANK
OFF_K_ROPE = OFF_KV_LAT + MLA_KV_RANK
CONV_WIDTH = 31
ROPE_THETA = 10000.0
NORM_EPS = 1e-6
LN_EPS = 1e-5

LANES = 128
V7X_VMEM_LIMIT = 56 * 1024 * 1024
MOD_ROWS = 8
NA_Q_ROWS = 8
NA_K_ROWS = NA_Q_ROWS + NA_WIN_H
NA_V_CHUNK = 256
NA_HEADS_PER_STEP = 2
VT_ROWS_PER_STEP = 2048
NEG_INF = float("-inf")
LOG2E = math.log2(math.e)


def _pick(dim, pref, align):
    if dim <= pref:
        return dim
    for t in range(pref - pref % align, 0, -align):
        if dim % t == 0:
            return t
    raise ValueError(f"no tile for {dim}")


def _params(*sem):
    return pltpu.CompilerParams(dimension_semantics=sem, vmem_limit_bytes=V7X_VMEM_LIMIT)


def _mod_body(s_ref, w_ref, b_ref, o_ref):
    s = s_ref[...]
    s = s * jax.nn.sigmoid(s)
    o_ref[0] = jnp.dot(s.astype(BF16), w_ref[0].astype(BF16), preferred_element_type=F32) + b_ref[0]


def _modulation(svec, ada_w, ada_b):
    depth, d, n = ada_w.shape
    tn = _pick(n, 512, LANES)
    return pl.pallas_call(
        _mod_body,
        grid=(depth, n // tn),
        in_specs=[pl.BlockSpec((MOD_ROWS, d), lambda l, j: (0, 0)),
                  pl.BlockSpec((1, d, tn), lambda l, j: (l, 0, j)),
                  pl.BlockSpec((1, 1, tn), lambda l, j: (l, 0, j))],
        out_specs=pl.BlockSpec((1, MOD_ROWS, tn), lambda l, j: (l, 0, j)),
        out_shape=jax.ShapeDtypeStruct((depth, MOD_ROWS, n), F32),
        compiler_params=_params("arbitrary", "arbitrary"),
        name="modulation",
    )(svec, ada_w, ada_b.reshape(depth, 1, n))


def _norm_body(*refs, row, modulated):
    if modulated:
        x_ref, g_ref, shift_ref, scale_ref, o_ref = refs
    else:
        x_ref, g_ref, o_ref = refs
    x = x_ref[...].astype(F32)
    ms = jnp.mean(x * x, axis=-1, keepdims=True)
    y = x * lax.rsqrt(ms + NORM_EPS) * g_ref[...]
    if modulated:
        y = y * (1.0 + scale_ref[0, row:row + 1, :]) + shift_ref[0, row:row + 1, :]
    o_ref[...] = y.astype(o_ref.dtype)


def _rms_norm(x, g, mod=None, layer=0, row=0, shift_chunk=0, scale_chunk=1):
    m, d = x.shape
    tm = _pick(m, 512, 16)
    in_specs = [pl.BlockSpec((tm, d), lambda i: (i, 0)), pl.BlockSpec((1, d), lambda i: (0, 0))]
    args = [x, g.reshape(1, d)]
    if mod is not None:
        in_specs += [pl.BlockSpec((1, MOD_ROWS, d), lambda i: (layer, 0, shift_chunk)),
                     pl.BlockSpec((1, MOD_ROWS, d), lambda i: (layer, 0, scale_chunk))]
        args += [mod, mod]
    return pl.pallas_call(
        functools.partial(_norm_body, row=row, modulated=mod is not None),
        grid=(m // tm,),
        in_specs=in_specs,
        out_specs=pl.BlockSpec((tm, d), lambda i: (i, 0)),
        out_shape=jax.ShapeDtypeStruct((m, d), BF16),
        compiler_params=_params("arbitrary"),
        name="rms_norm",
    )(*args)


def _head_norm(acc, g):
    outs = []
    for c in range(acc.shape[1] // HEAD_DIM):
        blk = acc[:, c * HEAD_DIM:(c + 1) * HEAD_DIM]
        ms = jnp.mean(blk * blk, axis=-1, keepdims=True)
        outs.append(blk * lax.rsqrt(ms + NORM_EPS) * g[:, c * HEAD_DIM:(c + 1) * HEAD_DIM])
    return outs[0] if len(outs) == 1 else jnp.concatenate(outs, axis=-1)


def _epilogue(kind, acc, acc2, ex, gate_row):
    if kind == "plain":
        return acc
    if kind == "relu2":
        r = jnp.maximum(acc, 0.0)
        return r * r
    if kind == "headnorm":
        return _head_norm(acc, ex["gain"][...])
    if kind == "glu":
        b = ex["bias"][...]
        b2 = ex["bias2"][...]
        return (acc + b) * jax.nn.sigmoid(acc2 + b2)
    if kind == "resid":
        y = acc
        if "bias" in ex:
            y = y + ex["bias"][...]
        return ex["res"][...] + ex["gate"][0, gate_row:gate_row + 1, :] * y
    raise ValueError(kind)


def _mm_body(*refs, kind, names, dual, k_steps, gate_row):
    a_ref, w_ref = refs[0], refs[1]
    pos = 2
    w2_ref = None
    if dual:
        w2_ref = refs[pos]
        pos += 1
    ex = dict(zip(names, refs[pos:pos + len(names)]))
    pos += len(names)
    o_ref = refs[pos]
    scratch = refs[pos + 1:]

    a = a_ref[...]
    if k_steps == 1:
        part = jnp.dot(a, w_ref[...].astype(BF16), preferred_element_type=F32)
        part2 = jnp.dot(a, w2_ref[...].astype(BF16), preferred_element_type=F32) if dual else None
        o_ref[...] = _epilogue(kind, part, part2, ex, gate_row).astype(o_ref.dtype)
        return

    acc_ref = scratch[0] if scratch else o_ref
    k = pl.program_id(2)

    @pl.when(k == 0)
    def _():
        acc_ref[...] = jnp.zeros(acc_ref.shape, F32)

    acc_ref[...] += jnp.dot(a, w_ref[...].astype(BF16), preferred_element_type=F32)

    @pl.when(k == k_steps - 1)
    def _():
        o_ref[...] = _epilogue(kind, acc_ref[...], None, ex, gate_row).astype(o_ref.dtype)


def _matmul(a, w, *, off=0, n=None, kind="plain", out_dtype=BF16, off2=None,
            gain=None, bias=None, bias2=None, res=None, gate=None, gate_layer=0, gate_row=0,
            gate_chunk=0, w_layer=None, tm_pref=1024, tn_pref=512, tk_pref=4096):
    m, kdim = a.shape
    n = w.shape[-1] if n is None else n
    tm = _pick(m, tm_pref, 16)
    tn = _pick(n, tn_pref, LANES)
    tk = _pick(kdim, tk_pref, LANES)
    assert off % tn == 0 and (off2 is None or off2 % tn == 0)
    k_steps = kdim // tk
    dual = off2 is not None
    assert not (dual and k_steps > 1)
    ob, ob2 = off // tn, (off2 // tn if dual else 0)

    def w_spec(col0):
        if w_layer is None:
            return pl.BlockSpec((tk, tn), lambda i, j, k: (k, col0 + j))
        return pl.BlockSpec((None, tk, tn), lambda i, j, k: (w_layer, k, col0 + j))

    in_specs = [pl.BlockSpec((tm, tk), lambda i, j, k: (i, k)), w_spec(ob)]
    args = [a, w]
    if dual:
        in_specs.append(w_spec(ob2))
        args.append(w)
    names = []
    for name, vec in (("gain", gain), ("bias", bias), ("bias2", bias2)):
        if vec is not None:
            names.append(name)
            in_specs.append(pl.BlockSpec((1, tn), lambda i, j, k: (0, j)))
            args.append(vec.reshape(1, n).astype(F32))
    if res is not None:
        names.append("res")
        in_specs.append(pl.BlockSpec((tm, tn), lambda i, j, k: (i, j)))
        args.append(res)
    if gate is not None:
        names.append("gate")
        gb = gate_chunk * (n // tn)
        in_specs.append(pl.BlockSpec((1, MOD_ROWS, tn), lambda i, j, k: (gate_layer, 0, gb + j)))
        args.append(gate)

    scratch = [pltpu.VMEM((tm, tn), F32)] if k_steps > 1 and out_dtype != F32 else []
    return pl.pallas_call(
        functools.partial(_mm_body, kind=kind, names=tuple(names), dual=dual, k_steps=k_steps,
                          gate_row=gate_row),
        grid=(m // tm, n // tn, k_steps),
        in_specs=in_specs,
        out_specs=pl.BlockSpec((tm, tn), lambda i, j, k: (i, j)),
        out_shape=jax.ShapeDtypeStruct((m, n), out_dtype),
        scratch_shapes=scratch,
        compiler_params=_params("parallel", "parallel", "arbitrary"),
        name="matmul_" + kind,
    )(*args)


def _mla_head_body(nope_ref, rope_ref, gn_ref, gr_ref, cos_ref, sin_ref, o_ref, *, shared_rope,
                   out_scale):
    lane = lax.broadcasted_iota(jnp.int32, (1, LANES), 1)
    first_half = lane < MLA_ROPE // 2
    cos = cos_ref[...]
    sin = sin_ref[...]
    gn = gn_ref[...]
    gr = gr_ref[...]
    ones = jnp.ones((LANES, LANES), BF16)
    for h in range(MLA_HEADS):
        nope = nope_ref[:, h * MLA_NOPE:(h + 1) * MLA_NOPE].astype(F32)
        if shared_rope:
            rope = rope_ref[...].astype(F32)
        else:
            rope = rope_ref[:, h * LANES:(h + 1) * LANES].astype(F32)
        sq = nope * nope + rope * rope
        hi = sq.astype(BF16)
        lo = (sq - hi.astype(F32)).astype(BF16)
        ss = (jnp.dot(hi, ones, preferred_element_type=F32)
              + jnp.dot(lo, ones, preferred_element_type=F32))
        inv = lax.rsqrt(ss * (1.0 / MLA_QK) + NORM_EPS)
        nope = nope * inv * gn
        rope = rope * inv * gr
        partner = jnp.where(first_half, pltpu.roll(rope, LANES - MLA_ROPE // 2, 1),
                            pltpu.roll(rope, MLA_ROPE // 2, 1))
        rope = rope * cos + partner * sin
        o_ref[h, :, 0:MLA_NOPE] = (nope * out_scale).astype(o_ref.dtype)
        o_ref[h, :, MLA_NOPE:] = (rope * out_scale).astype(o_ref.dtype)


def _mla_heads(nope_src, nope_off, rope_src, rope_off, shared_rope, g, cos, sin, out_scale):
    m = nope_src.shape[0]
    tm = _pick(m, 256, 16)
    g_nope = g[:MLA_NOPE].reshape(1, MLA_NOPE).astype(F32)
    g_rope = jnp.pad(g[MLA_NOPE:], (0, LANES - MLA_ROPE)).reshape(1, LANES).astype(F32)
    nw = MLA_HEADS * MLA_NOPE
    nb = nope_off // nw
    if shared_rope:
        rope_spec = pl.BlockSpec((tm, LANES), lambda i: (i, 0))
    else:
        rw = MLA_HEADS * LANES
        rb = rope_off // rw
        rope_spec = pl.BlockSpec((tm, rw), lambda i: (i, rb))
    return pl.pallas_call(
        functools.partial(_mla_head_body, shared_rope=shared_rope, out_scale=out_scale),
        grid=(m // tm,),
        in_specs=[pl.BlockSpec((tm, nw), lambda i: (i, nb)), rope_spec,
                  pl.BlockSpec((1, MLA_NOPE), lambda i: (0, 0)),
                  pl.BlockSpec((1, LANES), lambda i: (0, 0)),
                  pl.BlockSpec((tm, LANES), lambda i: (i, 0)),
                  pl.BlockSpec((tm, LANES), lambda i: (i, 0))],
        out_specs=pl.BlockSpec((MLA_HEADS, tm, 2 * LANES), lambda i: (0, i, 0)),
        out_shape=jax.ShapeDtypeStruct((MLA_HEADS, m, 2 * LANES), BF16),
        compiler_params=_params("arbitrary"),
        name="mla_head_norm_rope",
    )(nope_src, rope_src, g_nope, g_rope, cos, sin)


FLASH_TQ = 1024
FLASH_TK = 2816
_NT_DIMS = (((1,), (1,)), ((), ()))
FLASH_ONES_ROWS = 16
SOFTMAX_SAFE_BOUND = 60.0
BF16_NORM_SLACK = 1.01


def _flash_body(bound_ref, q_ref, k_ref, vt_ref, _, o_ref, *, n_chunks, tkc):
    q = q_ref[0]
    tq = q.shape[0]
    bound = bound_ref[0]

    def scores(c):
        k_c = k_ref[0, c * tkc:(c + 1) * tkc, :]
        return lax.dot_general(k_c, q, _NT_DIMS, preferred_element_type=F32)

    def finish(acc):
        out = acc[:MLA_V] * (1.0 / acc[MLA_V:MLA_V + 1])
        o_ref[...] = out.T.astype(o_ref.dtype)

    @pl.when(bound <= SOFTMAX_SAFE_BOUND)
    def _():
        acc = jnp.zeros((vt_ref.shape[2], tq), F32)
        for c in range(n_chunks):
            p = jnp.exp2(scores(c) - bound).astype(BF16)
            acc = acc + jnp.dot(vt_ref[0, c], p, preferred_element_type=F32)
        finish(acc)

    @pl.when(bound > SOFTMAX_SAFE_BOUND)
    def _():
        m = jnp.full((1, tq), NEG_INF, F32)
        acc = jnp.zeros((vt_ref.shape[2], tq), F32)
        for c in range(n_chunks):
            s = scores(c)
            m_new = jnp.maximum(m, jnp.max(s, axis=0, keepdims=True))
            alpha = jnp.exp2(m - m_new)
            p = jnp.exp2(s - m_new).astype(BF16)
            acc = alpha * acc + jnp.dot(vt_ref[0, c], p, preferred_element_type=F32)
            m = m_new
        finish(acc)


def _mla_attention(q, k, v_t, o_buf, o_off, bound):
    h, s, dq = q.shape
    t = k.shape[1]
    n_chunks, dva, tkc = v_t.shape[1:]
    dv = dva - FLASH_ONES_ROWS
    tq = _pick(s, FLASH_TQ, LANES)
    ob = o_off // dv
    return pl.pallas_call(
        functools.partial(_flash_body, n_chunks=n_chunks, tkc=tkc),
        grid=(h, s // tq),
        in_specs=[pl.BlockSpec(memory_space=pltpu.SMEM),
                  pl.BlockSpec((1, tq, dq), lambda hh, i: (hh, i, 0)),
                  pl.BlockSpec((1, t, dq), lambda hh, i: (hh, 0, 0)),
                  pl.BlockSpec((1, n_chunks, dva, tkc), lambda hh, i: (hh, 0, 0, 0)),
                  pl.BlockSpec(memory_space=pl.ANY)],
        out_specs=pl.BlockSpec((tq, dv), lambda hh, i: (i, ob + hh)),
        out_shape=jax.ShapeDtypeStruct(o_buf.shape, o_buf.dtype),
        input_output_aliases={4: 0},
        compiler_params=_params("parallel", "arbitrary"),
        name="mla_flash_attention",
    )(bound, q, k, v_t, o_buf)


def _na_variant_rows(rows, variant):
    r0 = (0, NA_Q_ROWS, rows - NA_Q_ROWS)[variant]
    ks = min(max(r0 - NA_WIN_H // 2, 0), rows - NA_K_ROWS)
    return r0, ks


def _na_bias_body(rpb_ref, o_ref, *, rows):
    h = pl.program_id(0)
    pair = 2 * GRID_W
    kc = lax.broadcasted_iota(jnp.int32, (GRID_W, pair), 0)
    lane = lax.broadcasted_iota(jnp.int32, (GRID_W, pair), 1)
    qc = jnp.where(lane < GRID_W, lane, lane - GRID_W)
    c_start = jnp.clip(qc - NA_WIN_W // 2, 0, GRID_W - NA_WIN_W)
    in_cols = (kc >= c_start) & (kc < c_start + NA_WIN_W)
    diag = kc - qc + (NA_WIN_W - 1)
    masked = jnp.full((GRID_W, pair), NEG_INF, F32)
    n_dr = 2 * NA_WIN_H - 1
    tiles = []
    for dr in range(n_dr):
        t = jnp.zeros((GRID_W, pair), F32)
        for jj in range(2 * NA_WIN_W - 1):
            t = jnp.where(diag == jj, rpb_ref[h, dr * (2 * NA_WIN_W - 1) + jj] * LOG2E, t)
        tiles.append(jnp.where(in_cols, t, masked))

    def row_tile(variant, a, b):
        r0, ks = _na_variant_rows(rows, variant)
        i, j = r0 + a, ks + b
        r_start = min(max(i - NA_WIN_H // 2, 0), rows - NA_WIN_H)
        if r_start <= j < r_start + NA_WIN_H:
            return tiles[j - i + NA_WIN_H - 1]
        return masked

    left = lane < GRID_W
    for variant in range(3):
        for b in range(NA_K_ROWS):
            for p in range(NA_Q_ROWS // 2):
                t0, t1 = row_tile(variant, 2 * p, b), row_tile(variant, 2 * p + 1, b)
                tile = t0 if t0 is t1 else jnp.where(left, t0, t1)
                o_ref[0, variant, b * GRID_W:(b + 1) * GRID_W, p * pair:(p + 1) * pair] = tile


def _na_bias_table(rpb, rows):
    h = rpb.shape[0]
    nq, nk = NA_Q_ROWS * GRID_W, NA_K_ROWS * GRID_W
    return pl.pallas_call(
        functools.partial(_na_bias_body, rows=rows),
        grid=(h,),
        in_specs=[pl.BlockSpec(memory_space=pltpu.SMEM)],
        out_specs=pl.BlockSpec((1, 3, nk, nq), lambda hh: (hh, 0, 0, 0)),
        out_shape=jax.ShapeDtypeStruct((h, 3, nk, nq), F32),
        compiler_params=_params("arbitrary"),
        name="na_bias_table",
    )(rpb.reshape(h, -1))


def _na_body(bound_ref, q_ref, k_ref, vt_ref, kc_ref, vct_ref, b_ref, _, o_ref, *, rows):
    rb = pl.program_id(1)
    ks = jnp.clip(rb * NA_Q_ROWS - NA_WIN_H // 2, 0, rows - NA_K_ROWS)
    start = pl.multiple_of(ks * GRID_W, NA_V_CHUNK)
    cs = lax.div(ks, NA_V_CHUNK // GRID_W)
    nk = NA_K_ROWS * GRID_W
    upper, width = bound_ref[0], bound_ref[1]

    def head(g, use_bound):
        cols = slice(g * HEAD_DIM, (g + 1) * HEAD_DIM)
        q = q_ref[:, cols]
        s_win = lax.dot_general(k_ref[pl.ds(start, nk), cols], q, _NT_DIMS, preferred_element_type=F32)
        s_win = s_win + b_ref[g, 0]
        s_ctx = lax.dot_general(kc_ref[:, cols], q, _NT_DIMS, preferred_element_type=F32)
        if use_bound:
            m = upper
        else:
            m = jnp.maximum(jnp.max(s_win, axis=0, keepdims=True), jnp.max(s_ctx, axis=0, keepdims=True))
        acc = jnp.dot(vct_ref[g], jnp.exp2(s_ctx - m).astype(BF16), preferred_element_type=F32)
        p_win = jnp.exp2(s_win - m).astype(BF16)
        for i in range(nk // NA_V_CHUNK):
            acc = acc + jnp.dot(vt_ref[g, cs + i], p_win[i * NA_V_CHUNK:(i + 1) * NA_V_CHUNK],
                                preferred_element_type=F32)
        out = acc[:HEAD_DIM] * (1.0 / acc[HEAD_DIM:HEAD_DIM + 1])
        o_ref[:, cols] = out.T.astype(o_ref.dtype)

    @pl.when(width <= 2.0 * SOFTMAX_SAFE_BOUND)
    def _():
        for g in range(NA_HEADS_PER_STEP):
            head(g, True)

    @pl.when(width > 2.0 * SOFTMAX_SAFE_BOUND)
    def _():
        for g in range(NA_HEADS_PER_STEP):
            head(g, False)


def _vt_body(v_ref, o_ref, *, chunk, per_step):
    dv = v_ref.shape[1]
    row = lax.broadcasted_iota(jnp.int32, (FLASH_ONES_ROWS, chunk), 0)
    ones_rows = jnp.where(row == 0, 1.0, 0.0).astype(o_ref.dtype)
    for i in range(per_step):
        x = v_ref[i * chunk:(i + 1) * chunk, :].astype(F32)
        o_ref[0, i, 0:dv, :] = x.T.astype(o_ref.dtype)
        o_ref[0, i, dv:, :] = ones_rows


def _values_transposed(src, off, n_heads, dv, chunk):
    t = src.shape[0]
    n_chunks = t // chunk
    per_step = _pick(n_chunks, max(1, VT_ROWS_PER_STEP // chunk), 1)
    cb = off // dv
    return pl.pallas_call(
        functools.partial(_vt_body, chunk=chunk, per_step=per_step),
        grid=(n_heads, n_chunks // per_step),
        in_specs=[pl.BlockSpec((per_step * chunk, dv), lambda hh, r: (r, cb + hh))],
        out_specs=pl.BlockSpec((1, per_step, dv + FLASH_ONES_ROWS, chunk), lambda hh, r: (hh, r, 0, 0)),
        out_shape=jax.ShapeDtypeStruct((n_heads, n_chunks, dv + FLASH_ONES_ROWS, chunk), src.dtype),
        compiler_params=_params("arbitrary", "arbitrary"),
        name="values_transposed",
    )(src)


def _na_attention(qk, v, ck, cv, bias, rows, o_buf, bound):
    s = qk.shape[0]
    c = ck.shape[0]
    nq, nk = NA_Q_ROWS * GRID_W, NA_K_ROWS * GRID_W
    n_blk = rows // NA_Q_ROWS
    n_vc = s // NA_V_CHUNK
    assert nk % NA_V_CHUNK == 0 and ((rows - NA_K_ROWS) * GRID_W) % NA_V_CHUNK == 0
    v_t = _values_transposed(v, 0, NA_HEADS, HEAD_DIM, NA_V_CHUNK)
    vc_t = _values_transposed(cv, 0, NA_HEADS, HEAD_DIM, c)[:, 0]
    dva = HEAD_DIM + FLASH_ONES_ROWS

    def variant(rb):
        return jnp.where(rb == 0, 0, jnp.where(rb == n_blk - 1, 2, 1))

    hps = NA_HEADS_PER_STEP
    gw = hps * HEAD_DIM
    n_grp = NA_HEADS // hps
    return pl.pallas_call(
        functools.partial(_na_body, rows=rows),
        grid=(n_grp, n_blk),
        in_specs=[pl.BlockSpec(memory_space=pltpu.SMEM),
                  pl.BlockSpec((nq, gw), lambda hg, rb: (rb, hg)),
                  pl.BlockSpec((s, gw), lambda hg, rb: (0, n_grp + hg)),
                  pl.BlockSpec((hps, n_vc, dva, NA_V_CHUNK), lambda hg, rb: (hg, 0, 0, 0)),
                  pl.BlockSpec((c, gw), lambda hg, rb: (0, hg)),
                  pl.BlockSpec((hps, dva, c), lambda hg, rb: (hg, 0, 0)),
                  pl.BlockSpec((hps, 1, nk, nq), lambda hg, rb: (hg, variant(rb), 0, 0)),
                  pl.BlockSpec(memory_space=pl.ANY)],
        out_specs=pl.BlockSpec((nq, gw), lambda hg, rb: (rb, hg)),
        out_shape=jax.ShapeDtypeStruct(o_buf.shape, o_buf.dtype),
        input_output_aliases={7: 0},
        compiler_params=_params("parallel", "arbitrary"),
        name="neighbourhood_attention",
    )(bound, qk, qk, v_t, ck, vc_t, bias, o_buf)


CONV_HALO = 16


SUBLANES = 8


def _dwconv_body(prev_ref, cur_ref, next_ref, w_ref, bdw_ref, g_ref, b_ref, o_ref, win_ref, conv_ref,
                 *, n_blk):
    i = pl.program_id(0)
    tm, d = cur_ref.shape
    nc = d // LANES
    has_prev = i > 0
    has_next = i < n_blk - 1
    for c in range(nc):
        sl = slice(c * LANES, (c + 1) * LANES)
        win_ref[c, 0:CONV_HALO, :] = jnp.where(has_prev, prev_ref[:, sl].astype(F32), 0.0)
        win_ref[c, CONV_HALO:CONV_HALO + tm, :] = cur_ref[:, sl].astype(F32)
        win_ref[c, CONV_HALO + tm:, :] = jnp.where(has_next, next_ref[:, sl].astype(F32), 0.0)

    groups = (CONV_WIDTH + SUBLANES) // SUBLANES
    span = tm + SUBLANES * (groups - 1)

    def chunk(c, carry):
        w = w_ref[c]
        acc = jnp.zeros((tm, LANES), F32) + bdw_ref[c]
        for b in range(SUBLANES):
            shifted = win_ref[c, pl.ds(b, span), :]
            for a in range(groups):
                o = SUBLANES * a + b
                if 1 <= o <= CONV_WIDTH:
                    acc = acc + shifted[SUBLANES * a:SUBLANES * a + tm, :] * w[o - 1:o, :]
        conv_ref[c] = acc
        return carry

    lax.fori_loop(0, nc, chunk, 0)

    total = conv_ref[0]
    for c in range(1, nc):
        total = total + conv_ref[c]
    mu = jnp.sum(total, axis=-1, keepdims=True) * (1.0 / d)
    sq = jnp.zeros((tm, LANES), F32)
    for c in range(nc):
        dlt = conv_ref[c] - mu
        sq = sq + dlt * dlt
    rstd = lax.rsqrt(jnp.sum(sq, axis=-1, keepdims=True) * (1.0 / d) + LN_EPS)
    for c in range(nc):
        y = (conv_ref[c] - mu) * rstd * g_ref[c] + b_ref[c]
        o_ref[:, c * LANES:(c + 1) * LANES] = (y * jax.nn.sigmoid(y)).astype(o_ref.dtype)


def _dwconv_ln_silu(u, w_dw, b_dw, g_ln, b_ln):
    m, d = u.shape
    tm = _pick(m, 128, CONV_HALO)
    nc = d // LANES
    n_blk = m // tm
    per = tm // CONV_HALO
    last_halo = m // CONV_HALO - 1
    assert CONV_WIDTH // 2 < CONV_HALO and CONV_WIDTH + 1 <= 2 * CONV_HALO
    vec = lambda v: v.reshape(nc, 1, LANES).astype(F32)
    w_chunks = jnp.transpose(w_dw.astype(F32).reshape(CONV_WIDTH, nc, LANES), (1, 0, 2))
    whole = lambda shape: pl.BlockSpec(shape, lambda i: (0,) * len(shape))
    return pl.pallas_call(
        functools.partial(_dwconv_body, n_blk=n_blk),
        grid=(n_blk,),
        in_specs=[pl.BlockSpec((CONV_HALO, d), lambda i: (jnp.maximum(i * per - 1, 0), 0)),
                  pl.BlockSpec((tm, d), lambda i: (i, 0)),
                  pl.BlockSpec((CONV_HALO, d), lambda i: (jnp.minimum((i + 1) * per, last_halo), 0)),
                  whole((nc, CONV_WIDTH, LANES)), whole((nc, 1, LANES)), whole((nc, 1, LANES)),
                  whole((nc, 1, LANES))],
        out_specs=pl.BlockSpec((tm, d), lambda i: (i, 0)),
        out_shape=jax.ShapeDtypeStruct((m, d), BF16),
        scratch_shapes=[pltpu.VMEM((nc, tm + 2 * CONV_HALO, LANES), F32),
                        pltpu.VMEM((nc, tm, LANES), F32)],
        compiler_params=_params("arbitrary"),
        name="dwconv_ln_silu",
    )(u, u, u, w_chunks, vec(b_dw), vec(g_ln), vec(b_ln))


def _rope_tables(n_ctx, n_tok):
    t = jnp.arange(n_tok, dtype=jnp.int32)
    pos = jnp.stack([t // GRID_W, t % GRID_W], axis=-1).astype(F32)
    n_freq = MLA_ROPE // 4
    inv_freq = ROPE_THETA ** (-jnp.arange(n_freq, dtype=F32) / n_freq)
    ang = (pos[:, :, None] * inv_freq).reshape(n_tok, 2 * n_freq)
    cos, sin = jnp.cos(ang), jnp.sin(ang)
    zeros = jnp.zeros((n_tok, LANES - MLA_ROPE), F32)
    cos_lat = jnp.concatenate([cos, cos, zeros], axis=-1)
    sin_lat = jnp.concatenate([-sin, sin, zeros], axis=-1)
    cos_ctx = jnp.concatenate([jnp.ones((n_ctx, MLA_ROPE), F32), jnp.zeros((n_ctx, LANES - MLA_ROPE), F32)], -1)
    sin_ctx = jnp.zeros((n_ctx, LANES), F32)
    return (jnp.concatenate([cos_ctx, cos_lat], 0), jnp.concatenate([sin_ctx, sin_lat], 0))


def _attention_layer(x, xc, mod, layer, j, p):
    s, d = x.shape
    c = xc.shape[0]
    rows = s // GRID_W
    assert s % GRID_W == 0 and rows % NA_Q_ROWS == 0 and rows >= NA_K_ROWS
    w_in = p["att_w_in"]

    hl = _rms_norm(x, p["g_mix"][layer], mod, layer, 0, 0, 1)
    hc = _rms_norm(xc, p["g_mix"][layer], mod, layer, 1, 0, 1)

    g_q = jnp.tile(p["att_g_na_q"][j], NA_HEADS) * (HEAD_DIM ** -0.5 * LOG2E)
    g_k = jnp.tile(p["att_g_na_k"][j], NA_HEADS)
    w_kr = jnp.pad(w_in[j, :, OFF_K_ROPE:], ((0, 0), (0, LANES - MLA_ROPE)))

    qk = _matmul(hl, w_in, w_layer=j, off=0, n=2 * NA_WIDTH, kind="headnorm",
                 gain=jnp.concatenate([g_q, g_k]))
    na_v = _matmul(hl, w_in, w_layer=j, off=OFF_NA_V, n=NA_WIDTH)
    q_lat = _matmul(hl, w_in, w_layer=j, off=OFF_Q_LAT, n=MLA_Q_RANK)
    kv_lat = _matmul(hl, w_in, w_layer=j, off=OFF_KV_LAT, n=MLA_KV_RANK)
    k_rope = _matmul(hl, w_kr, tn_pref=LANES)
    c_k = _matmul(hc, w_in, w_layer=j, off=OFF_NA_K, n=NA_WIDTH, kind="headnorm", gain=g_k)
    c_v = _matmul(hc, w_in, w_layer=j, off=OFF_NA_V, n=NA_WIDTH)
    c_kv_lat = _matmul(hc, w_in, w_layer=j, off=OFF_KV_LAT, n=MLA_KV_RANK)
    c_k_rope = _matmul(hc, w_kr, tn_pref=LANES)

    w_qb = p["att_w_qb"][j].reshape(MLA_Q_RANK, MLA_HEADS, MLA_QK)
    w_qb_rope = jnp.pad(w_qb[:, :, MLA_NOPE:], ((0, 0), (0, 0), (0, LANES - MLA_ROPE)))
    w_qb_perm = jnp.concatenate([w_qb[:, :, :MLA_NOPE].reshape(MLA_Q_RANK, -1),
                                 w_qb_rope.reshape(MLA_Q_RANK, -1)], axis=-1)
    w_kvb = p["att_w_kvb"][j].reshape(MLA_KV_RANK, MLA_HEADS, 2, MLA_NOPE)
    w_kvb_perm = jnp.transpose(w_kvb, (0, 2, 1, 3)).reshape(MLA_KV_RANK, -1)

    q_raw = _matmul(_rms_norm(q_lat, p["att_g_qa"][j]), w_qb_perm, tn_pref=1024)
    kv_n = _rms_norm(jnp.concatenate([c_kv_lat, kv_lat], axis=0), p["att_g_kva"][j])
    kv_raw = _matmul(kv_n, w_kvb_perm, tn_pref=4096)
    kr_all = jnp.concatenate([c_k_rope, k_rope], axis=0)

    cos, sin = _rope_tables(c, s)
    q_full = _mla_heads(q_raw, 0, q_raw, MLA_HEADS * MLA_NOPE, False, p["att_g_mla_q"][j],
                        cos[c:], sin[c:], MLA_QK ** -0.5 * LOG2E)
    k_full = _mla_heads(kv_raw, 0, kr_all, 0, True, p["att_g_mla_k"][j], cos, sin, 1.0)
    tkc = _pick(c + s, FLASH_TK, LANES)
    v_t = _values_transposed(kv_raw, MLA_HEADS * MLA_NOPE, MLA_HEADS, MLA_V, tkc)

    bias = _na_bias_table(p["att_na_rpb"][j], rows)
    o = jnp.zeros((s, NA_WIDTH + MLA_HEADS * MLA_V), BF16)
    rpb_log2 = p["att_na_rpb"][j] * LOG2E
    qk_cap = (HEAD_DIM * HEAD_DIM ** -0.5 * LOG2E * BF16_NORM_SLACK) * (
        jnp.max(jnp.abs(p["att_g_na_q"][j])) * jnp.max(jnp.abs(p["att_g_na_k"][j])))
    rpb_hi = jnp.maximum(jnp.max(rpb_log2), 0.0)
    rpb_lo = jnp.minimum(jnp.min(rpb_log2), 0.0)
    na_bound = jnp.stack([qk_cap + rpb_hi, 2.0 * qk_cap + rpb_hi - rpb_lo]).astype(F32)
    o = _na_attention(qk, na_v, c_k, c_v, bias, rows, o, na_bound)
    bound = (MLA_QK * MLA_QK ** -0.5 * LOG2E * BF16_NORM_SLACK) * (
        jnp.max(jnp.abs(p["att_g_mla_q"][j])) * jnp.max(jnp.abs(p["att_g_mla_k"][j])))
    o = _mla_attention(q_full, k_full, v_t, o, NA_WIDTH, bound.reshape(1).astype(F32))
    return _matmul(o, p["att_w_out"], w_layer=j, kind="resid", out_dtype=F32, res=x, gate=mod,
                   gate_layer=layer, gate_row=0, gate_chunk=2)


def _conv_layer(x, mod, layer, j, p):
    d = x.shape[1]
    hl = _rms_norm(x, p["g_mix"][layer], mod, layer, 0, 0, 1)
    b1 = p["conv_b_pw1"][j]
    u = _matmul(hl, p["conv_w_pw1"], w_layer=j, off=0, off2=d, n=d, kind="glu", bias=b1[:d],
                bias2=b1[d:], tn_pref=256)
    v = _dwconv_ln_silu(u, p["conv_w_dw"][j], p["conv_b_dw"][j], p["conv_g_ln"][j], p["conv_b_ln"][j])
    return _matmul(v, p["conv_w_pw2"], w_layer=j, kind="resid", out_dtype=F32, bias=p["conv_b_pw2"][j],
                   res=x, gate=mod, gate_layer=layer, gate_row=0, gate_chunk=2)


def _mlp(x, mod, layer, p):
    h = _rms_norm(x, p["g_mlp"][layer], mod, layer, 0, 3, 4)
    u = _matmul(h, p["mlp_w1"], w_layer=layer, kind="relu2")
    return _matmul(u, p["mlp_w2"], w_layer=layer, kind="resid", out_dtype=F32, res=x, gate=mod,
                   gate_layer=layer, gate_row=0, gate_chunk=5, tm_pref=2048, tn_pref=1024, tk_pref=1024)


def kernel(x, c, ctx, c_ctx, ada_w, ada_b, g_mix, g_mlp, mlp_w1, mlp_w2, att_w_in, att_g_qa, att_w_qb, att_g_kva, att_w_kvb, att_g_na_q, att_g_na_k, att_na_rpb, att_g_mla_q, att_g_mla_k, att_w_out, conv_w_pw1, conv_b_pw1, conv_w_dw, conv_b_dw, conv_g_ln, conv_b_ln, conv_w_pw2, conv_b_pw2):
    p = dict(g_mix=g_mix, g_mlp=g_mlp, mlp_w1=mlp_w1, mlp_w2=mlp_w2, att_w_in=att_w_in,
             att_g_qa=att_g_qa, att_w_qb=att_w_qb, att_g_kva=att_g_kva, att_w_kvb=att_w_kvb,
             att_g_na_q=att_g_na_q, att_g_na_k=att_g_na_k, att_na_rpb=att_na_rpb,
             att_g_mla_q=att_g_mla_q, att_g_mla_k=att_g_mla_k, att_w_out=att_w_out,
             conv_w_pw1=conv_w_pw1, conv_b_pw1=conv_b_pw1, conv_w_dw=conv_w_dw, conv_b_dw=conv_b_dw,
             conv_g_ln=conv_g_ln, conv_b_ln=conv_b_ln, conv_w_pw2=conv_w_pw2, conv_b_pw2=conv_b_pw2)
    batch, _, d = x.shape
    depth = ada_w.shape[0]
    assert batch == 1 and c.shape[0] == 1
    svec = jnp.zeros((MOD_ROWS, d), F32).at[0].set(c[0]).at[1].set(c_ctx)
    mod = _modulation(svec, ada_w, ada_b)

    xl = x[0]
    xc = ctx[0]
    for layer in range(depth):
        j = layer // 2
        if layer % 2 == 0:
            assert not any(m % 2 == 0 for m in range(layer + 1, depth))
            xl = _attention_layer(xl, xc, mod, layer, j, p)
        else:
            xl = _conv_layer(xl, mod, layer, j, p)
        xl = _mlp(xl, mod, layer, p)
    return xl[None]
```

```python
import functools
import math

import jax
import jax.numpy as jnp
from jax import lax
from jax.experimental import pallas as pl
from jax.experimental.pallas import tpu as pltpu

F32 = jnp.float32
BF16 = jnp.bfloat16

GRID_W = 64
NA_HEADS = 16
HEAD_DIM = 128
NA_WIN_H = 8
NA_WIN_W = 16
NA_WIDTH = NA_HEADS * HEAD_DIM
MLA_HEADS = 16
MLA_Q_RANK = 1536
MLA_KV_RANK = 512
MLA_NOPE = 128
MLA_ROPE = 64
MLA_V = 128
MLA_QK = MLA_NOPE + MLA_ROPE
OFF_NA_K = NA_WIDTH
OFF_NA_V = 2 * NA_WIDTH
OFF_Q_LAT = 3 * NA_WIDTH
OFF_KV_LAT = OFF_Q_LAT + MLA_Q_RANK
OFF_K_ROPE = OFF_KV_LAT + MLA_KV_RANK
CONV_WIDTH = 31
ROPE_THETA = 10000.0
NORM_EPS = 1e-6
LN_EPS = 1e-5

LANES = 128
V7X_VMEM_LIMIT = 56 * 1024 * 1024
MOD_ROWS = 8
NA_Q_ROWS = 8
NA_K_ROWS = NA_Q_ROWS + NA_WIN_H
NA_V_CHUNK = 256
NA_HEADS_PER_STEP = 2
VT_ROWS_PER_STEP = 2048
NEG_INF = float("-inf")
LOG2E = math.log2(math.e)


def _pick(dim, pref, align):
    if dim <= pref:
        return dim
    for t in range(pref - pref % align, 0, -align):
        if dim % t == 0:
            return t
    raise ValueError(f"no tile for {dim}")


def _params(*sem):
    return pltpu.CompilerParams(dimension_semantics=sem, vmem_limit_bytes=V7X_VMEM_LIMIT)


def _mod_body(s_ref, w_ref, b_ref, o_ref):
    s = s_ref[...]
    s = s * jax.nn.sigmoid(s)
    o_ref[0] = jnp.dot(s.astype(BF16), w_ref[0].astype(BF16), preferred_element_type=F32) + b_ref[0]


def _modulation(svec, ada_w, ada_b):
    depth, d, n = ada_w.shape
    tn = _pick(n, 512, LANES)
    return pl.pallas_call(
        _mod_body,
        grid=(depth, n // tn),
        in_specs=[pl.BlockSpec((MOD_ROWS, d), lambda l, j: (0, 0)),
                  pl.BlockSpec((1, d, tn), lambda l, j: (l, 0, j)),
                  pl.BlockSpec((1, 1, tn), lambda l, j: (l, 0, j))],
        out_specs=pl.BlockSpec((1, MOD_ROWS, tn), lambda l, j: (l, 0, j)),
        out_shape=jax.ShapeDtypeStruct((depth, MOD_ROWS, n), F32),
        compiler_params=_params("arbitrary", "arbitrary"),
        name="modulation",
    )(svec, ada_w, ada_b.reshape(depth, 1, n))


def _norm_body(*refs, row, modulated):
    if modulated:
        x_ref, g_ref, shift_ref, scale_ref, o_ref = refs
    else:
        x_ref, g_ref, o_ref = refs
    x = x_ref[...].astype(F32)
    ms = jnp.mean(x * x, axis=-1, keepdims=True)
    y = x * lax.rsqrt(ms + NORM_EPS) * g_ref[...]
    if modulated:
        y = y * (1.0 + scale_ref[0, row:row + 1, :]) + shift_ref[0, row:row + 1, :]
    o_ref[...] = y.astype(o_ref.dtype)


def _rms_norm(x, g, mod=None, layer=0, row=0, shift_chunk=0, scale_chunk=1):
    m, d = x.shape
    tm = _pick(m, 512, 16)
    in_specs = [pl.BlockSpec((tm, d), lambda i: (i, 0)), pl.BlockSpec((1, d), lambda i: (0, 0))]
    args = [x, g.reshape(1, d)]
    if mod is not None:
        in_specs += [pl.BlockSpec((1, MOD_ROWS, d), lambda i: (layer, 0, shift_chunk)),
                     pl.BlockSpec((1, MOD_ROWS, d), lambda i: (layer, 0, scale_chunk))]
        args += [mod, mod]
    return pl.pallas_call(
        functools.partial(_norm_body, row=row, modulated=mod is not None),
        grid=(m // tm,),
        in_specs=in_specs,
        out_specs=pl.BlockSpec((tm, d), lambda i: (i, 0)),
        out_shape=jax.ShapeDtypeStruct((m, d), BF16),
        compiler_params=_params("arbitrary"),
        name="rms_norm",
    )(*args)


def _head_norm(acc, g):
    outs = []
    for c in range(acc.shape[1] // HEAD_DIM):
        blk = acc[:, c * HEAD_DIM:(c + 1) * HEAD_DIM]
        ms = jnp.mean(blk * blk, axis=-1, keepdims=True)
        outs.append(blk * lax.rsqrt(ms + NORM_EPS) * g[:, c * HEAD_DIM:(c + 1) * HEAD_DIM])
    return outs[0] if len(outs) == 1 else jnp.concatenate(outs, axis=-1)


def _epilogue(kind, acc, acc2, ex, gate_row):
    if kind == "plain":
        return acc
    if kind == "relu2":
        r = jnp.maximum(acc, 0.0)
        return r * r
    if kind == "headnorm":
        return _head_norm(acc, ex["gain"][...])
    if kind == "glu":
        b = ex["bias"][...]
        b2 = ex["bias2"][...]
        return (acc + b) * jax.nn.sigmoid(acc2 + b2)
    if kind == "resid":
        y = acc
        if "bias" in ex:
            y = y + ex["bias"][...]
        return ex["res"][...] + ex["gate"][0, gate_row:gate_row + 1, :] * y
    raise ValueError(kind)


def _mm_body(*refs, kind, names, dual, k_steps, gate_row, w_transposed):
    a_ref, w_ref = refs[0], refs[1]
    pos = 2
    w2_ref = None
    if dual:
        w2_ref = refs[pos]
        pos += 1
    ex = dict(zip(names, refs[pos:pos + len(names)]))
    pos += len(names)
    o_ref = refs[pos]
    scratch = refs[pos + 1:]

    def mm(lhs, w_tile_ref):
        w_tile = w_tile_ref[...].astype(BF16)
        if w_transposed:
            return lax.dot_general(lhs, w_tile, _NT_DIMS, preferred_element_type=F32)
        return jnp.dot(lhs, w_tile, preferred_element_type=F32)

    a = a_ref[...]
    if k_steps == 1:
        part = mm(a, w_ref)
        if kind == "pad_lanes":
            n = part.shape[1]
            o_ref[:, :n] = part.astype(o_ref.dtype)
            o_ref[:, n:] = jnp.zeros((part.shape[0], o_ref.shape[1] - n), o_ref.dtype)
            return
        part2 = mm(a, w2_ref) if dual else None
        o_ref[...] = _epilogue(kind, part, part2, ex, gate_row).astype(o_ref.dtype)
        return

    acc_ref = scratch[0] if scratch else o_ref
    k = pl.program_id(2)

    @pl.when(k == 0)
    def _():
        acc_ref[...] = jnp.zeros(acc_ref.shape, F32)

    acc_ref[...] += mm(a, w_ref)

    @pl.when(k == k_steps - 1)
    def _():
        o_ref[...] = _epilogue(kind, acc_ref[...], None, ex, gate_row).astype(o_ref.dtype)


def _matmul(a, w, *, off=0, n=None, kind="plain", out_dtype=BF16, off2=None,
            gain=None, bias=None, bias2=None, res=None, gate=None, gate_layer=0, gate_row=0,
            gate_chunk=0, w_layer=None, w_transposed=False, tm_pref=1024, tn_pref=512, tk_pref=4096):
    m, kdim = a.shape
    n = w.shape[-2 if w_transposed else -1] if n is None else n
    tm = _pick(m, tm_pref, 16)
    tn = _pick(n, tn_pref, LANES)
    tk = _pick(kdim, tk_pref, LANES)
    assert off % tn == 0 and (off2 is None or off2 % tn == 0)
    k_steps = kdim // tk
    dual = off2 is not None
    assert not (dual and k_steps > 1)
    ob, ob2 = off // tn, (off2 // tn if dual else 0)
    pad_lanes = kind == "pad_lanes"
    assert not pad_lanes or (n == tn < LANES and k_steps == 1)
    n_out, tn_out = (LANES, LANES) if pad_lanes else (n, tn)

    def w_spec(col0):
        if w_transposed:
            block, index = (tn, tk), lambda i, j, k: (col0 + j, k)
        else:
            block, index = (tk, tn), lambda i, j, k: (k, col0 + j)
        if w_layer is None:
            return pl.BlockSpec(block, index)
        return pl.BlockSpec((None,) + block, lambda i, j, k: (w_layer,) + index(i, j, k))

    in_specs = [pl.BlockSpec((tm, tk), lambda i, j, k: (i, k)), w_spec(ob)]
    args = [a, w]
    if dual:
        in_specs.append(w_spec(ob2))
        args.append(w)
    names = []
    for name, vec in (("gain", gain), ("bias", bias), ("bias2", bias2)):
        if vec is not None:
            names.append(name)
            in_specs.append(pl.BlockSpec((1, tn), lambda i, j, k: (0, j)))
            args.append(vec.reshape(1, n).astype(F32))
    if res is not None:
        names.append("res")
        in_specs.append(pl.BlockSpec((tm, tn), lambda i, j, k: (i, j)))
        args.append(res)
    if gate is not None:
        names.append("gate")
        gb = gate_chunk * (n // tn)
        in_specs.append(pl.BlockSpec((1, MOD_ROWS, tn), lambda i, j, k: (gate_layer, 0, gb + j)))
        args.append(gate)

    scratch = [pltpu.VMEM((tm, tn), F32)] if k_steps > 1 and out_dtype != F32 else []
    return pl.pallas_call(
        functools.partial(_mm_body, kind=kind, names=tuple(names), dual=dual, k_steps=k_steps,
                          gate_row=gate_row, w_transposed=w_transposed),
        grid=(m // tm, n // tn, k_steps),
        in_specs=in_specs,
        out_specs=pl.BlockSpec((tm, tn_out), lambda i, j, k: (i, j)),
        out_shape=jax.ShapeDtypeStruct((m, n_out), out_dtype),
        scratch_shapes=scratch,
        compiler_params=_params("parallel", "parallel", "arbitrary"),
        name="matmul_" + kind,
    )(*args)


def _mla_head_body(nope_ref, rope_ref, gn_ref, gr_ref, cos_ref, sin_ref, o_ref, *, shared_rope,
                   out_scale, nope_stride):
    lane = lax.broadcasted_iota(jnp.int32, (1, LANES), 1)
    first_half = lane < MLA_ROPE // 2
    cos = cos_ref[...]
    sin = sin_ref[...]
    gn = gn_ref[...]
    gr = gr_ref[...]
    ones = jnp.ones((LANES, LANES), BF16)
    for h in range(MLA_HEADS):
        nope = nope_ref[:, h * nope_stride:h * nope_stride + MLA_NOPE].astype(F32)
        if shared_rope:
            rope = rope_ref[...].astype(F32)
        else:
            rope = rope_ref[:, h * LANES:(h + 1) * LANES].astype(F32)
        sq = nope * nope + rope * rope
        hi = sq.astype(BF16)
        lo = (sq - hi.astype(F32)).astype(BF16)
        ss = (jnp.dot(hi, ones, preferred_element_type=F32)
              + jnp.dot(lo, ones, preferred_element_type=F32))
        inv = lax.rsqrt(ss * (1.0 / MLA_QK) + NORM_EPS)
        nope = nope * inv * gn
        rope = rope * inv * gr
        partner = jnp.where(first_half, pltpu.roll(rope, LANES - MLA_ROPE // 2, 1),
                            pltpu.roll(rope, MLA_ROPE // 2, 1))
        rope = rope * cos + partner * sin
        o_ref[h, :, 0:MLA_NOPE] = (nope * out_scale).astype(o_ref.dtype)
        o_ref[h, :, MLA_NOPE:] = (rope * out_scale).astype(o_ref.dtype)


def _mla_heads(nope_src, nope_off, nope_stride, rope_src, rope_off, shared_rope, g, cos, sin, out_scale):
    m = nope_src.shape[0]
    tm = _pick(m, 256, 16)
    g_nope = g[:MLA_NOPE].reshape(1, MLA_NOPE).astype(F32)
    g_rope = jnp.pad(g[MLA_NOPE:], (0, LANES - MLA_ROPE)).reshape(1, LANES).astype(F32)
    nw = MLA_HEADS * nope_stride
    nb = nope_off // nw
    if shared_rope:
        rope_spec = pl.BlockSpec((tm, LANES), lambda i: (i, 0))
    else:
        rw = MLA_HEADS * LANES
        rb = rope_off // rw
        rope_spec = pl.BlockSpec((tm, rw), lambda i: (i, rb))
    return pl.pallas_call(
        functools.partial(_mla_head_body, shared_rope=shared_rope, out_scale=out_scale,
                          nope_stride=nope_stride),
        grid=(m // tm,),
        in_specs=[pl.BlockSpec((tm, nw), lambda i: (i, nb)), rope_spec,
                  pl.BlockSpec((1, MLA_NOPE), lambda i: (0, 0)),
                  pl.BlockSpec((1, LANES), lambda i: (0, 0)),
                  pl.BlockSpec((tm, LANES), lambda i: (i, 0)),
                  pl.BlockSpec((tm, LANES), lambda i: (i, 0))],
        out_specs=pl.BlockSpec((MLA_HEADS, tm, 2 * LANES), lambda i: (0, i, 0)),
        out_shape=jax.ShapeDtypeStruct((MLA_HEADS, m, 2 * LANES), BF16),
        compiler_params=_params("arbitrary"),
        name="mla_head_norm_rope",
    )(nope_src, rope_src, g_nope, g_rope, cos, sin)


FLASH_TQ = 1024
FLASH_TK = 2816
_NT_DIMS = (((1,), (1,)), ((), ()))
FLASH_ONES_ROWS = 16
SOFTMAX_SAFE_BOUND = 60.0
BF16_NORM_SLACK = 1.01


def _flash_body(bound_ref, q_ref, k_ref, vt_ref, _, o_ref, *, n_chunks, tkc):
    q = q_ref[0]
    tq = q.shape[0]
    bound = bound_ref[0]

    def scores(c):
        k_c = k_ref[0, c * tkc:(c + 1) * tkc, :]
        return lax.dot_general(k_c, q, _NT_DIMS, preferred_element_type=F32)

    def finish(acc):
        out = acc[:MLA_V] * (1.0 / acc[MLA_V:MLA_V + 1])
        o_ref[...] = out.T.astype(o_ref.dtype)

    @pl.when(bound <= SOFTMAX_SAFE_BOUND)
    def _():
        acc = jnp.zeros((vt_ref.shape[2], tq), F32)
        for c in range(n_chunks):
            p = jnp.exp2(scores(c) - bound).astype(BF16)
            acc = acc + jnp.dot(vt_ref[0, c], p, preferred_element_type=F32)
        finish(acc)

    @pl.when(bound > SOFTMAX_SAFE_BOUND)
    def _():
        m = jnp.full((1, tq), NEG_INF, F32)
        acc = jnp.zeros((vt_ref.shape[2], tq), F32)
        for c in range(n_chunks):
            s = scores(c)
            m_new = jnp.maximum(m, jnp.max(s, axis=0, keepdims=True))
            alpha = jnp.exp2(m - m_new)
            p = jnp.exp2(s - m_new).astype(BF16)
            acc = alpha * acc + jnp.dot(vt_ref[0, c], p, preferred_element_type=F32)
            m = m_new
        finish(acc)


def _mla_attention(q, k, v_t, o_buf, o_off, bound):
    h, s, dq = q.shape
    t = k.shape[1]
    n_chunks, dva, tkc = v_t.shape[1:]
    dv = dva - FLASH_ONES_ROWS
    tq = _pick(s, FLASH_TQ, LANES)
    ob = o_off // dv
    return pl.pallas_call(
        functools.partial(_flash_body, n_chunks=n_chunks, tkc=tkc),
        grid=(h, s // tq),
        in_specs=[pl.BlockSpec(memory_space=pltpu.SMEM),
                  pl.BlockSpec((1, tq, dq), lambda hh, i: (hh, i, 0)),
                  pl.BlockSpec((1, t, dq), lambda hh, i: (hh, 0, 0)),
                  pl.BlockSpec((1, n_chunks, dva, tkc), lambda hh, i: (hh, 0, 0, 0)),
                  pl.BlockSpec(memory_space=pl.ANY)],
        out_specs=pl.BlockSpec((tq, dv), lambda hh, i: (i, ob + hh)),
        out_shape=jax.ShapeDtypeStruct(o_buf.shape, o_buf.dtype),
        input_output_aliases={4: 0},
        compiler_params=_params("parallel", "arbitrary"),
        name="mla_flash_attention",
    )(bound, q, k, v_t, o_buf)


def _na_variant_rows(rows, variant):
    r0 = (0, NA_Q_ROWS, rows - NA_Q_ROWS)[variant]
    ks = min(max(r0 - NA_WIN_H // 2, 0), rows - NA_K_ROWS)
    return r0, ks


def _na_bias_body(rpb_ref, o_ref, *, rows):
    h = pl.program_id(0)
    pair = 2 * GRID_W
    kc = lax.broadcasted_iota(jnp.int32, (GRID_W, pair), 0)
    lane = lax.broadcasted_iota(jnp.int32, (GRID_W, pair), 1)
    qc = jnp.where(lane < GRID_W, lane, lane - GRID_W)
    c_start = jnp.clip(qc - NA_WIN_W // 2, 0, GRID_W - NA_WIN_W)
    in_cols = (kc >= c_start) & (kc < c_start + NA_WIN_W)
    diag = kc - qc + (NA_WIN_W - 1)
    masked = jnp.full((GRID_W, pair), NEG_INF, F32)
    n_dr = 2 * NA_WIN_H - 1
    tiles = []
    for dr in range(n_dr):
        t = jnp.zeros((GRID_W, pair), F32)
        for jj in range(2 * NA_WIN_W - 1):
            t = jnp.where(diag == jj, rpb_ref[h, dr * (2 * NA_WIN_W - 1) + jj] * LOG2E, t)
        tiles.append(jnp.where(in_cols, t, masked))

    def row_tile(variant, a, b):
        r0, ks = _na_variant_rows(rows, variant)
        i, j = r0 + a, ks + b
        r_start = min(max(i - NA_WIN_H // 2, 0), rows - NA_WIN_H)
        if r_start <= j < r_start + NA_WIN_H:
            return tiles[j - i + NA_WIN_H - 1]
        return masked

    left = lane < GRID_W
    for variant in range(3):
        for b in range(NA_K_ROWS):
            for p in range(NA_Q_ROWS // 2):
                t0, t1 = row_tile(variant, 2 * p, b), row_tile(variant, 2 * p + 1, b)
                tile = t0 if t0 is t1 else jnp.where(left, t0, t1)
                o_ref[0, variant, b * GRID_W:(b + 1) * GRID_W, p * pair:(p + 1) * pair] = tile


def _na_bias_table(rpb, rows):
    h = rpb.shape[0]
    nq, nk = NA_Q_ROWS * GRID_W, NA_K_ROWS * GRID_W
    return pl.pallas_call(
        functools.partial(_na_bias_body, rows=rows),
        grid=(h,),
        in_specs=[pl.BlockSpec(memory_space=pltpu.SMEM)],
        out_specs=pl.BlockSpec((1, 3, nk, nq), lambda hh: (hh, 0, 0, 0)),
        out_shape=jax.ShapeDtypeStruct((h, 3, nk, nq), F32),
        compiler_params=_params("arbitrary"),
        name="na_bias_table",
    )(rpb.reshape(h, -1))


def _na_body(bound_ref, q_ref, k_ref, vt_ref, kc_ref, vct_ref, b_ref, _, o_ref, *, rows):
    rb = pl.program_id(1)
    ks = jnp.clip(rb * NA_Q_ROWS - NA_WIN_H // 2, 0, rows - NA_K_ROWS)
    start = pl.multiple_of(ks * GRID_W, NA_V_CHUNK)
    cs = lax.div(ks, NA_V_CHUNK // GRID_W)
    nk = NA_K_ROWS * GRID_W
    upper, width = bound_ref[0], bound_ref[1]

    def head(g, use_bound):
        cols = slice(g * HEAD_DIM, (g + 1) * HEAD_DIM)
        q = q_ref[:, cols]
        s_win = lax.dot_general(k_ref[pl.ds(start, nk), cols], q, _NT_DIMS, preferred_element_type=F32)
        s_win = s_win + b_ref[g, 0]
        s_ctx = lax.dot_general(kc_ref[:, cols], q, _NT_DIMS, preferred_element_type=F32)
        if use_bound:
            m = upper
        else:
            m = jnp.maximum(jnp.max(s_win, axis=0, keepdims=True), jnp.max(s_ctx, axis=0, keepdims=True))
        acc = jnp.dot(vct_ref[g], jnp.exp2(s_ctx - m).astype(BF16), preferred_element_type=F32)
        p_win = jnp.exp2(s_win - m).astype(BF16)
        for i in range(nk // NA_V_CHUNK):
            acc = acc + jnp.dot(vt_ref[g, cs + i], p_win[i * NA_V_CHUNK:(i + 1) * NA_V_CHUNK],
                                preferred_element_type=F32)
        out = acc[:HEAD_DIM] * (1.0 / acc[HEAD_DIM:HEAD_DIM + 1])
        o_ref[:, cols] = out.T.astype(o_ref.dtype)

    @pl.when(width <= 2.0 * SOFTMAX_SAFE_BOUND)
    def _():
        for g in range(NA_HEADS_PER_STEP):
            head(g, True)

    @pl.when(width > 2.0 * SOFTMAX_SAFE_BOUND)
    def _():
        for g in range(NA_HEADS_PER_STEP):
            head(g, False)


def _vt_body(v_ref, o_ref, *, chunk, per_step):
    dv = v_ref.shape[1]
    row = lax.broadcasted_iota(jnp.int32, (FLASH_ONES_ROWS, chunk), 0)
    ones_rows = jnp.where(row == 0, 1.0, 0.0).astype(o_ref.dtype)
    for i in range(per_step):
        x = v_ref[i * chunk:(i + 1) * chunk, :].astype(F32)
        o_ref[0, i, 0:dv, :] = x.T.astype(o_ref.dtype)
        o_ref[0, i, dv:, :] = ones_rows


def _values_transposed(src, off, stride, n_heads, dv, chunk):
    t = src.shape[0]
    n_chunks = t // chunk
    per_step = _pick(n_chunks, max(1, VT_ROWS_PER_STEP // chunk), 1)
    assert off % dv == 0 and stride % dv == 0
    cb, cs = off // dv, stride // dv
    return pl.pallas_call(
        functools.partial(_vt_body, chunk=chunk, per_step=per_step),
        grid=(n_heads, n_chunks // per_step),
        in_specs=[pl.BlockSpec((per_step * chunk, dv), lambda hh, r: (r, cb + cs * hh))],
        out_specs=pl.BlockSpec((1, per_step, dv + FLASH_ONES_ROWS, chunk), lambda hh, r: (hh, r, 0, 0)),
        out_shape=jax.ShapeDtypeStruct((n_heads, n_chunks, dv + FLASH_ONES_ROWS, chunk), src.dtype),
        compiler_params=_params("arbitrary", "arbitrary"),
        name="values_transposed",
    )(src)


def _na_attention(qk, v, ck, cv, bias, rows, o_buf, bound):
    s = qk.shape[0]
    c = ck.shape[0]
    nq, nk = NA_Q_ROWS * GRID_W, NA_K_ROWS * GRID_W
    n_blk = rows // NA_Q_ROWS
    n_vc = s // NA_V_CHUNK
    assert nk % NA_V_CHUNK == 0 and ((rows - NA_K_ROWS) * GRID_W) % NA_V_CHUNK == 0
    v_t = _values_transposed(v, 0, HEAD_DIM, NA_HEADS, HEAD_DIM, NA_V_CHUNK)
    vc_t = _values_transposed(cv, 0, HEAD_DIM, NA_HEADS, HEAD_DIM, c)[:, 0]
    dva = HEAD_DIM + FLASH_ONES_ROWS

    def variant(rb):
        return jnp.where(rb == 0, 0, jnp.where(rb == n_blk - 1, 2, 1))

    hps = NA_HEADS_PER_STEP
    gw = hps * HEAD_DIM
    n_grp = NA_HEADS // hps
    return pl.pallas_call(
        functools.partial(_na_body, rows=rows),
        grid=(n_grp, n_blk),
        in_specs=[pl.BlockSpec(memory_space=pltpu.SMEM),
                  pl.BlockSpec((nq, gw), lambda hg, rb: (rb, hg)),
                  pl.BlockSpec((s, gw), lambda hg, rb: (0, n_grp + hg)),
                  pl.BlockSpec((hps, n_vc, dva, NA_V_CHUNK), lambda hg, rb: (hg, 0, 0, 0)),
                  pl.BlockSpec((c, gw), lambda hg, rb: (0, hg)),
                  pl.BlockSpec((hps, dva, c), lambda hg, rb: (hg, 0, 0)),
                  pl.BlockSpec((hps, 1, nk, nq), lambda hg, rb: (hg, variant(rb), 0, 0)),
                  pl.BlockSpec(memory_space=pl.ANY)],
        out_specs=pl.BlockSpec((nq, gw), lambda hg, rb: (rb, hg)),
        out_shape=jax.ShapeDtypeStruct(o_buf.shape, o_buf.dtype),
        input_output_aliases={7: 0},
        compiler_params=_params("parallel", "arbitrary"),
        name="neighbourhood_attention",
    )(bound, qk, qk, v_t, ck, vc_t, bias, o_buf)


CONV_HALO = 16


SUBLANES = 8


def _dwconv_body(prev_ref, cur_ref, next_ref, w_ref, bdw_ref, g_ref, b_ref, o_ref, win_ref, conv_ref,
                 *, n_blk):
    i = pl.program_id(0)
    tm, d = cur_ref.shape
    nc = d // LANES
    has_prev = i > 0
    has_next = i < n_blk - 1
    for c in range(nc):
        sl = slice(c * LANES, (c + 1) * LANES)
        win_ref[c, 0:CONV_HALO, :] = jnp.where(has_prev, prev_ref[:, sl].astype(F32), 0.0)
        win_ref[c, CONV_HALO:CONV_HALO + tm, :] = cur_ref[:, sl].astype(F32)
        win_ref[c, CONV_HALO + tm:, :] = jnp.where(has_next, next_ref[:, sl].astype(F32), 0.0)

    groups = (CONV_WIDTH + SUBLANES) // SUBLANES
    span = tm + SUBLANES * (groups - 1)

    def chunk(c, carry):
        w = w_ref[c]
        acc = jnp.zeros((tm, LANES), F32) + bdw_ref[c]
        for b in range(SUBLANES):
            shifted = win_ref[c, pl.ds(b, span), :]
            for a in range(groups):
                o = SUBLANES * a + b
                if 1 <= o <= CONV_WIDTH:
                    acc = acc + shifted[SUBLANES * a:SUBLANES * a + tm, :] * w[o - 1:o, :]
        conv_ref[c] = acc
        return carry

    lax.fori_loop(0, nc, chunk, 0)

    total = conv_ref[0]
    for c in range(1, nc):
        total = total + conv_ref[c]
    mu = jnp.sum(total, axis=-1, keepdims=True) * (1.0 / d)
    sq = jnp.zeros((tm, LANES), F32)
    for c in range(nc):
        dlt = conv_ref[c] - mu
        sq = sq + dlt * dlt
    rstd = lax.rsqrt(jnp.sum(sq, axis=-1, keepdims=True) * (1.0 / d) + LN_EPS)
    for c in range(nc):
        y = (conv_ref[c] - mu) * rstd * g_ref[c] + b_ref[c]
        o_ref[:, c * LANES:(c + 1) * LANES] = (y * jax.nn.sigmoid(y)).astype(o_ref.dtype)


def _dwconv_ln_silu(u, w_dw, b_dw, g_ln, b_ln):
    m, d = u.shape
    tm = _pick(m, 128, CONV_HALO)
    nc = d // LANES
    n_blk = m // tm
    per = tm // CONV_HALO
    last_halo = m // CONV_HALO - 1
    assert CONV_WIDTH // 2 < CONV_HALO and CONV_WIDTH + 1 <= 2 * CONV_HALO
    vec = lambda v: v.reshape(nc, 1, LANES).astype(F32)
    w_chunks = jnp.transpose(w_dw.astype(F32).reshape(CONV_WIDTH, nc, LANES), (1, 0, 2))
    whole = lambda shape: pl.BlockSpec(shape, lambda i: (0,) * len(shape))
    return pl.pallas_call(
        functools.partial(_dwconv_body, n_blk=n_blk),
        grid=(n_blk,),
        in_specs=[pl.BlockSpec((CONV_HALO, d), lambda i: (jnp.maximum(i * per - 1, 0), 0)),
                  pl.BlockSpec((tm, d), lambda i: (i, 0)),
                  pl.BlockSpec((CONV_HALO, d), lambda i: (jnp.minimum((i + 1) * per, last_halo), 0)),
                  whole((nc, CONV_WIDTH, LANES)), whole((nc, 1, LANES)), whole((nc, 1, LANES)),
                  whole((nc, 1, LANES))],
        out_specs=pl.BlockSpec((tm, d), lambda i: (i, 0)),
        out_shape=jax.ShapeDtypeStruct((m, d), BF16),
        scratch_shapes=[pltpu.VMEM((nc, tm + 2 * CONV_HALO, LANES), F32),
                        pltpu.VMEM((nc, tm, LANES), F32)],
        compiler_params=_params("arbitrary"),
        name="dwconv_ln_silu",
    )(u, u, u, w_chunks, vec(b_dw), vec(g_ln), vec(b_ln))


def _rope_tables(n_ctx, n_tok):
    t = jnp.arange(n_tok, dtype=jnp.int32)
    pos = jnp.stack([t // GRID_W, t % GRID_W], axis=-1).astype(F32)
    n_freq = MLA_ROPE // 4
    inv_freq = ROPE_THETA ** (-jnp.arange(n_freq, dtype=F32) / n_freq)
    ang = (pos[:, :, None] * inv_freq).reshape(n_tok, 2 * n_freq)
    cos, sin = jnp.cos(ang), jnp.sin(ang)
    zeros = jnp.zeros((n_tok, LANES - MLA_ROPE), F32)
    cos_lat = jnp.concatenate([cos, cos, zeros], axis=-1)
    sin_lat = jnp.concatenate([-sin, sin, zeros], axis=-1)
    cos_ctx = jnp.concatenate([jnp.ones((n_ctx, MLA_ROPE), F32), jnp.zeros((n_ctx, LANES - MLA_ROPE), F32)], -1)
    sin_ctx = jnp.zeros((n_ctx, LANES), F32)
    return (jnp.concatenate([cos_ctx, cos_lat], 0), jnp.concatenate([sin_ctx, sin_lat], 0))


def _attention_layer(x, xc, mod, layer, j, p):
    s, d = x.shape
    c = xc.shape[0]
    rows = s // GRID_W
    assert s % GRID_W == 0 and rows % NA_Q_ROWS == 0 and rows >= NA_K_ROWS
    w_in = functools.partial(_matmul, w=jnp.swapaxes(p["att_w_in"], 1, 2), w_layer=j, w_transposed=True)

    hl = _rms_norm(x, p["g_mix"][layer], mod, layer, 0, 0, 1)
    hc = _rms_norm(xc, p["g_mix"][layer], mod, layer, 1, 0, 1)

    g_q = jnp.tile(p["att_g_na_q"][j], NA_HEADS) * (HEAD_DIM ** -0.5 * LOG2E)
    g_k = jnp.tile(p["att_g_na_k"][j], NA_HEADS)

    qk = w_in(hl, off=0, n=2 * NA_WIDTH, kind="headnorm", gain=jnp.concatenate([g_q, g_k]))
    na_v = w_in(hl, off=OFF_NA_V, n=NA_WIDTH)
    q_lat = w_in(hl, off=OFF_Q_LAT, n=MLA_Q_RANK)
    kv_lat = w_in(hl, off=OFF_KV_LAT, n=MLA_KV_RANK)
    k_rope = w_in(hl, off=OFF_K_ROPE, n=MLA_ROPE, kind="pad_lanes")
    c_k = w_in(hc, off=OFF_NA_K, n=NA_WIDTH, kind="headnorm", gain=g_k)
    c_v = w_in(hc, off=OFF_NA_V, n=NA_WIDTH)
    c_kv_lat = w_in(hc, off=OFF_KV_LAT, n=MLA_KV_RANK)
    c_k_rope = w_in(hc, off=OFF_K_ROPE, n=MLA_ROPE, kind="pad_lanes")

    w_qb = p["att_w_qb"][j].reshape(MLA_Q_RANK, MLA_HEADS, MLA_QK)
    w_qb_rope = jnp.pad(w_qb[:, :, MLA_NOPE:], ((0, 0), (0, 0), (0, LANES - MLA_ROPE)))
    w_qb_perm = jnp.concatenate([w_qb[:, :, :MLA_NOPE].reshape(MLA_Q_RANK, -1),
                                 w_qb_rope.reshape(MLA_Q_RANK, -1)], axis=-1)

    q_raw = _matmul(_rms_norm(q_lat, p["att_g_qa"][j]), w_qb_perm, tn_pref=1024)
    kv_n = _rms_norm(jnp.concatenate([c_kv_lat, kv_lat], axis=0), p["att_g_kva"][j])
    kv_raw = _matmul(kv_n, p["att_w_kvb"], w_layer=j, tn_pref=4096)
    kv_head = MLA_NOPE + MLA_V
    kr_all = jnp.concatenate([c_k_rope, k_rope], axis=0)

    cos, sin = _rope_tables(c, s)
    q_full = _mla_heads(q_raw, 0, MLA_NOPE, q_raw, MLA_HEADS * MLA_NOPE, False, p["att_g_mla_q"][j],
                        cos[c:], sin[c:], MLA_QK ** -0.5 * LOG2E)
    k_full = _mla_heads(kv_raw, 0, kv_head, kr_all, 0, True, p["att_g_mla_k"][j], cos, sin, 1.0)
    tkc = _pick(c + s, FLASH_TK, LANES)
    v_t = _values_transposed(kv_raw, MLA_NOPE, kv_head, MLA_HEADS, MLA_V, tkc)

    bias = _na_bias_table(p["att_na_rpb"][j], rows)
    o = jnp.zeros((s, NA_WIDTH + MLA_HEADS * MLA_V), BF16)
    rpb_log2 = p["att_na_rpb"][j] * LOG2E
    qk_cap = (HEAD_DIM * HEAD_DIM ** -0.5 * LOG2E * BF16_NORM_SLACK) * (
        jnp.max(jnp.abs(p["att_g_na_q"][j])) * jnp.max(jnp.abs(p["att_g_na_k"][j])))
    rpb_hi = jnp.maximum(jnp.max(rpb_log2), 0.0)
    rpb_lo = jnp.minimum(jnp.min(rpb_log2), 0.0)
    na_bound = jnp.stack([qk_cap + rpb_hi, 2.0 * qk_cap + rpb_hi - rpb_lo]).astype(F32)
    o = _na_attention(qk, na_v, c_k, c_v, bias, rows, o, na_bound)
    bound = (MLA_QK * MLA_QK ** -0.5 * LOG2E * BF16_NORM_SLACK) * (
        jnp.max(jnp.abs(p["att_g_mla_q"][j])) * jnp.max(jnp.abs(p["att_g_mla_k"][j])))
    o = _mla_attention(q_full, k_full, v_t, o, NA_WIDTH, bound.reshape(1).astype(F32))
    return _matmul(o, p["att_w_out"], w_layer=j, kind="resid", out_dtype=F32, res=x, gate=mod,
                   gate_layer=layer, gate_row=0, gate_chunk=2)


def _conv_layer(x, mod, layer, j, p):
    d = x.shape[1]
    hl = _rms_norm(x, p["g_mix"][layer], mod, layer, 0, 0, 1)
    b1 = p["conv_b_pw1"][j]
    u = _matmul(hl, p["conv_w_pw1"], w_layer=j, off=0, off2=d, n=d, kind="glu", bias=b1[:d],
                bias2=b1[d:], tn_pref=256)
    v = _dwconv_ln_silu(u, p["conv_w_dw"][j], p["conv_b_dw"][j], p["conv_g_ln"][j], p["conv_b_ln"][j])
    return _matmul(v, p["conv_w_pw2"], w_layer=j, kind="resid", out_dtype=F32, bias=p["conv_b_pw2"][j],
                   res=x, gate=mod, gate_layer=layer, gate_row=0, gate_chunk=2)


def _mlp(x, mod, layer, p):
    h = _rms_norm(x, p["g_mlp"][layer], mod, layer, 0, 3, 4)
    u = _matmul(h, p["mlp_w1"], w_layer=layer, kind="relu2")
    return _matmul(u, p["mlp_w2"], w_layer=layer, kind="resid", out_dtype=F32, res=x, gate=mod,
                   gate_layer=layer, gate_row=0, gate_chunk=5, tm_pref=2048, tn_pref=1024, tk_pref=1024)


def kernel(x, c, ctx, c_ctx, ada_w, ada_b, g_mix, g_mlp, mlp_w1, mlp_w2, att_w_in, att_g_qa, att_w_qb, att_g_kva, att_w_kvb, att_g_na_q, att_g_na_k, att_na_rpb, att_g_mla_q, att_g_mla_k, att_w_out, conv_w_pw1, conv_b_pw1, conv_w_dw, conv_b_dw, conv_g_ln, conv_b_ln, conv_w_pw2, conv_b_pw2):
    p = dict(g_mix=g_mix, g_mlp=g_mlp, mlp_w1=mlp_w1, mlp_w2=mlp_w2, att_w_in=att_w_in,
             att_g_qa=att_g_qa, att_w_qb=att_w_qb, att_g_kva=att_g_kva, att_w_kvb=att_w_kvb,
             att_g_na_q=att_g_na_q, att_g_na_k=att_g_na_k, att_na_rpb=att_na_rpb,
             att_g_mla_q=att_g_mla_q, att_g_mla_k=att_g_mla_k, att_w_out=att_w_out,
             conv_w_pw1=conv_w_pw1, conv_b_pw1=conv_b_pw1, conv_w_dw=conv_w_dw, conv_b_dw=conv_b_dw,
             conv_g_ln=conv_g_ln, conv_b_ln=conv_b_ln, conv_w_pw2=conv_w_pw2, conv_b_pw2=conv_b_pw2)
    batch, _, d = x.shape
    depth = ada_w.shape[0]
    assert batch == 1 and c.shape[0] == 1
    svec = jnp.zeros((MOD_ROWS, d), F32).at[0].set(c[0]).at[1].set(c_ctx)
    mod = _modulation(svec, ada_w, ada_b)

    xl = x[0]
    xc = ctx[0]
    for layer in range(depth):
        j = layer // 2
        if layer % 2 == 0:
            assert not any(m % 2 == 0 for m in range(layer + 1, depth))
            xl = _attention_layer(xl, xc, mod, layer, j, p)
        else:
            xl = _conv_layer(xl, mod, layer, j, p)
        xl = _mlp(xl, mod, layer, p)
    return xl[None]
```

```python
import functools
import math

import jax
import jax.numpy as jnp
from jax import lax
from jax.experimental import pallas as pl
from jax.experimental.pallas import tpu as pltpu

F32 = jnp.float32
BF16 = jnp.bfloat16

GRID_W = 64
NA_HEADS = 16
HEAD_DIM = 128
NA_WIN_H = 8
NA_WIN_W = 16
NA_WIDTH = NA_HEADS * HEAD_DIM
MLA_HEADS = 16
MLA_Q_RANK = 1536
MLA_KV_RANK = 512
MLA_NOPE = 128
MLA_ROPE = 64
MLA_V = 128
MLA_QK = MLA_NOPE + MLA_ROPE
OFF_NA_K = NA_WIDTH
OFF_NA_V = 2 * NA_WIDTH
OFF_Q_LAT = 3 * NA_WIDTH
OFF_KV_LAT = OFF_Q_LAT + MLA_Q_RANK
OFF_K_ROPE = OFF_KV_LAT + MLA_KV_RANK
CONV_WIDTH = 31
ROPE_THETA = 10000.0
NORM_EPS = 1e-6
LN_EPS = 1e-5

LANES = 128
SUBLANES = 8
V7X_VMEM_LIMIT = 56 * 1024 * 1024

MOD_ROWS = 8
NA_Q_ROWS = 8
NA_K_ROWS = NA_Q_ROWS + NA_WIN_H
NA_V_CHUNK = 256
NA_HEADS_PER_STEP = 2
VT_ROWS_PER_STEP = 2048
FLASH_TQ = 1024
FLASH_TK = 2816
FLASH_ONES_ROWS = 16
CONV_HALO = 16

NEG_INF = float("-inf")
LOG2E = math.log2(math.e)
_NT_DIMS = (((1,), (1,)), ((), ()))
SOFTMAX_SAFE_BOUND = 60.0
BF16_NORM_SLACK = 1.01


def _pick(dim, pref, align):
    if dim <= pref:
        return dim
    for t in range(pref - pref % align, 0, -align):
        if dim % t == 0:
            return t
    raise ValueError(f"no tile for {dim}")


def _params(*sem):
    return pltpu.CompilerParams(dimension_semantics=sem, vmem_limit_bytes=V7X_VMEM_LIMIT)


def _mod_body(s_ref, w_ref, b_ref, o_ref):
    s = s_ref[...]
    s = s * jax.nn.sigmoid(s)
    o_ref[0] = jnp.dot(s.astype(BF16), w_ref[0].astype(BF16), preferred_element_type=F32) + b_ref[0]


def _modulation(svec, ada_w, ada_b):
    depth, d, n = ada_w.shape
    tn = _pick(n, 512, LANES)
    return pl.pallas_call(
        _mod_body,
        grid=(depth, n // tn),
        in_specs=[pl.BlockSpec((MOD_ROWS, d), lambda l, j: (0, 0)),
                  pl.BlockSpec((1, d, tn), lambda l, j: (l, 0, j)),
                  pl.BlockSpec((1, 1, tn), lambda l, j: (l, 0, j))],
        out_specs=pl.BlockSpec((1, MOD_ROWS, tn), lambda l, j: (l, 0, j)),
        out_shape=jax.ShapeDtypeStruct((depth, MOD_ROWS, n), F32),
        compiler_params=_params("arbitrary", "arbitrary"),
        name="modulation",
    )(svec, ada_w, ada_b.reshape(depth, 1, n))


def _norm_body(*refs, row, modulated):
    if modulated:
        x_ref, g_ref, shift_ref, scale_ref, o_ref = refs
    else:
        x_ref, g_ref, o_ref = refs
    x = x_ref[...].astype(F32)
    ms = jnp.mean(x * x, axis=-1, keepdims=True)
    y = x * lax.rsqrt(ms + NORM_EPS) * g_ref[...]
    if modulated:
        y = y * (1.0 + scale_ref[0, row:row + 1, :]) + shift_ref[0, row:row + 1, :]
    o_ref[...] = y.astype(o_ref.dtype)


def _rms_norm(x, g, mod=None, layer=0, row=0, shift_chunk=0, scale_chunk=1):
    m, d = x.shape
    tm = _pick(m, 512, 16)
    in_specs = [pl.BlockSpec((tm, d), lambda i: (i, 0)), pl.BlockSpec((1, d), lambda i: (0, 0))]
    args = [x, g.reshape(1, d)]
    if mod is not None:
        in_specs += [pl.BlockSpec((1, MOD_ROWS, d), lambda i: (layer, 0, shift_chunk)),
                     pl.BlockSpec((1, MOD_ROWS, d), lambda i: (layer, 0, scale_chunk))]
        args += [mod, mod]
    return pl.pallas_call(
        functools.partial(_norm_body, row=row, modulated=mod is not None),
        grid=(m // tm,),
        in_specs=in_specs,
        out_specs=pl.BlockSpec((tm, d), lambda i: (i, 0)),
        out_shape=jax.ShapeDtypeStruct((m, d), BF16),
        compiler_params=_params("arbitrary"),
        name="rms_norm",
    )(*args)


def _head_norm(acc, g):
    outs = []
    for c in range(acc.shape[1] // HEAD_DIM):
        blk = acc[:, c * HEAD_DIM:(c + 1) * HEAD_DIM]
        ms = jnp.mean(blk * blk, axis=-1, keepdims=True)
        outs.append(blk * lax.rsqrt(ms + NORM_EPS) * g[:, c * HEAD_DIM:(c + 1) * HEAD_DIM])
    return outs[0] if len(outs) == 1 else jnp.concatenate(outs, axis=-1)


def _epilogue(kind, acc, acc2, ex, gate_row):
    if kind == "plain":
        return acc
    if kind == "relu2":
        r = jnp.maximum(acc, 0.0)
        return r * r
    if kind == "headnorm":
        return _head_norm(acc, ex["gain"][...])
    if kind == "glu":
        b = ex["bias"][...]
        b2 = ex["bias2"][...]
        return (acc + b) * jax.nn.sigmoid(acc2 + b2)
    if kind == "resid":
        y = acc
        if "bias" in ex:
            y = y + ex["bias"][...]
        return ex["res"][...] + ex["gate"][0, gate_row:gate_row + 1, :] * y
    raise ValueError(kind)


def _mm_body(*refs, kind, names, dual, k_steps, gate_row, w_transposed):
    a_ref, w_ref = refs[0], refs[1]
    pos = 2
    w2_ref = None
    if dual:
        w2_ref = refs[pos]
        pos += 1
    ex = dict(zip(names, refs[pos:pos + len(names)]))
    pos += len(names)
    o_ref = refs[pos]
    scratch = refs[pos + 1:]

    def mm(lhs, w_tile_ref):
        w_tile = w_tile_ref[...].astype(BF16)
        if w_transposed:
            return lax.dot_general(lhs, w_tile, _NT_DIMS, preferred_element_type=F32)
        return jnp.dot(lhs, w_tile, preferred_element_type=F32)

    a = a_ref[...]
    if k_steps == 1:
        part = mm(a, w_ref)
        if kind == "pad_lanes":
            n = part.shape[1]
            o_ref[:, :n] = part.astype(o_ref.dtype)
            o_ref[:, n:] = jnp.zeros((part.shape[0], o_ref.shape[1] - n), o_ref.dtype)
            return
        part2 = mm(a, w2_ref) if dual else None
        o_ref[...] = _epilogue(kind, part, part2, ex, gate_row).astype(o_ref.dtype)
        return

    acc_ref = scratch[0] if scratch else o_ref
    k = pl.program_id(2)

    @pl.when(k == 0)
    def _():
        acc_ref[...] = jnp.zeros(acc_ref.shape, F32)

    acc_ref[...] += mm(a, w_ref)

    @pl.when(k == k_steps - 1)
    def _():
        o_ref[...] = _epilogue(kind, acc_ref[...], None, ex, gate_row).astype(o_ref.dtype)


def _matmul(a, w, *, off=0, n=None, kind="plain", out_dtype=BF16, off2=None,
            gain=None, bias=None, bias2=None, res=None, gate=None, gate_layer=0, gate_row=0,
            gate_chunk=0, w_layer=None, w_transposed=False, tm_pref=1024, tn_pref=512, tk_pref=4096):
    m, kdim = a.shape
    n = w.shape[-2 if w_transposed else -1] if n is None else n
    tm = _pick(m, tm_pref, 16)
    tn = _pick(n, tn_pref, LANES)
    tk = _pick(kdim, tk_pref, LANES)
    assert off % tn == 0 and (off2 is None or off2 % tn == 0)
    k_steps = kdim // tk
    dual = off2 is not None
    assert not (dual and k_steps > 1)
    ob, ob2 = off // tn, (off2 // tn if dual else 0)
    pad_lanes = kind == "pad_lanes"
    assert not pad_lanes or (n == tn < LANES and k_steps == 1)
    n_out, tn_out = (LANES, LANES) if pad_lanes else (n, tn)

    def w_spec(col0):
        if w_transposed:
            block, index = (tn, tk), lambda i, j, k: (col0 + j, k)
        else:
            block, index = (tk, tn), lambda i, j, k: (k, col0 + j)
        if w_layer is None:
            return pl.BlockSpec(block, index)
        return pl.BlockSpec((None,) + block, lambda i, j, k: (w_layer,) + index(i, j, k))

    in_specs = [pl.BlockSpec((tm, tk), lambda i, j, k: (i, k)), w_spec(ob)]
    args = [a, w]
    if dual:
        in_specs.append(w_spec(ob2))
        args.append(w)
    names = []
    for name, vec in (("gain", gain), ("bias", bias), ("bias2", bias2)):
        if vec is not None:
            names.append(name)
            in_specs.append(pl.BlockSpec((1, tn), lambda i, j, k: (0, j)))
            args.append(vec.reshape(1, n).astype(F32))
    if res is not None:
        names.append("res")
        in_specs.append(pl.BlockSpec((tm, tn), lambda i, j, k: (i, j)))
        args.append(res)
    if gate is not None:
        names.append("gate")
        gb = gate_chunk * (n // tn)
        in_specs.append(pl.BlockSpec((1, MOD_ROWS, tn), lambda i, j, k: (gate_layer, 0, gb + j)))
        args.append(gate)

    scratch = [pltpu.VMEM((tm, tn), F32)] if k_steps > 1 and out_dtype != F32 else []
    return pl.pallas_call(
        functools.partial(_mm_body, kind=kind, names=tuple(names), dual=dual, k_steps=k_steps,
                          gate_row=gate_row, w_transposed=w_transposed),
        grid=(m // tm, n // tn, k_steps),
        in_specs=in_specs,
        out_specs=pl.BlockSpec((tm, tn_out), lambda i, j, k: (i, j)),
        out_shape=jax.ShapeDtypeStruct((m, n_out), out_dtype),
        scratch_shapes=scratch,
        compiler_params=_params("parallel", "parallel", "arbitrary"),
        name="matmul_" + kind,
    )(*args)


def _mla_head_body(nope_ref, rope_ref, gn_ref, gr_ref, cos_ref, sin_ref, o_ref, *, shared_rope,
                   out_scale, nope_stride):
    lane = lax.broadcasted_iota(jnp.int32, (1, LANES), 1)
    first_half = lane < MLA_ROPE // 2
    cos = cos_ref[...]
    sin = sin_ref[...]
    gn = gn_ref[...]
    gr = gr_ref[...]
    ones = jnp.ones((LANES, LANES), BF16)
    for h in range(MLA_HEADS):
        nope = nope_ref[:, h * nope_stride:h * nope_stride + MLA_NOPE].astype(F32)
        if shared_rope:
            rope = rope_ref[...].astype(F32)
        else:
            rope = rope_ref[:, h * LANES:(h + 1) * LANES].astype(F32)
        sq = nope * nope + rope * rope
        hi = sq.astype(BF16)
        lo = (sq - hi.astype(F32)).astype(BF16)
        ss = (jnp.dot(hi, ones, preferred_element_type=F32)
              + jnp.dot(lo, ones, preferred_element_type=F32))
        inv = lax.rsqrt(ss * (1.0 / MLA_QK) + NORM_EPS)
        nope = nope * inv * gn
        rope = rope * inv * gr
        partner = jnp.where(first_half, pltpu.roll(rope, LANES - MLA_ROPE // 2, 1),
                            pltpu.roll(rope, MLA_ROPE // 2, 1))
        rope = rope * cos + partner * sin
        o_ref[h, :, 0:MLA_NOPE] = (nope * out_scale).astype(o_ref.dtype)
        o_ref[h, :, MLA_NOPE:] = (rope * out_scale).astype(o_ref.dtype)


def _mla_heads(nope_src, nope_off, nope_stride, rope_src, rope_off, shared_rope, g, cos, sin, out_scale):
    m = nope_src.shape[0]
    tm = _pick(m, 256, 16)
    g_nope = g[:MLA_NOPE].reshape(1, MLA_NOPE).astype(F32)
    g_rope = jnp.pad(g[MLA_NOPE:], (0, LANES - MLA_ROPE)).reshape(1, LANES).astype(F32)
    nw = MLA_HEADS * nope_stride
    nb = nope_off // nw
    if shared_rope:
        rope_spec = pl.BlockSpec((tm, LANES), lambda i: (i, 0))
    else:
        rw = MLA_HEADS * LANES
        rb = rope_off // rw
        rope_spec = pl.BlockSpec((tm, rw), lambda i: (i, rb))
    return pl.pallas_call(
        functools.partial(_mla_head_body, shared_rope=shared_rope, out_scale=out_scale,
                          nope_stride=nope_stride),
        grid=(m // tm,),
        in_specs=[pl.BlockSpec((tm, nw), lambda i: (i, nb)), rope_spec,
                  pl.BlockSpec((1, MLA_NOPE), lambda i: (0, 0)),
                  pl.BlockSpec((1, LANES), lambda i: (0, 0)),
                  pl.BlockSpec((tm, LANES), lambda i: (i, 0)),
                  pl.BlockSpec((tm, LANES), lambda i: (i, 0))],
        out_specs=pl.BlockSpec((MLA_HEADS, tm, 2 * LANES), lambda i: (0, i, 0)),
        out_shape=jax.ShapeDtypeStruct((MLA_HEADS, m, 2 * LANES), BF16),
        compiler_params=_params("arbitrary"),
        name="mla_head_norm_rope",
    )(nope_src, rope_src, g_nope, g_rope, cos, sin)


def _flash_body(bound_ref, q_ref, k_ref, vt_ref, _, o_ref, *, n_chunks, tkc):
    q = q_ref[0]
    tq = q.shape[0]
    bound = bound_ref[0]

    def scores(c):
        k_c = k_ref[0, c * tkc:(c + 1) * tkc, :]
        return lax.dot_general(k_c, q, _NT_DIMS, preferred_element_type=F32)

    def finish(acc):
        out = acc[:MLA_V] * (1.0 / acc[MLA_V:MLA_V + 1])
        o_ref[...] = out.T.astype(o_ref.dtype)

    @pl.when(bound <= SOFTMAX_SAFE_BOUND)
    def _():
        acc = jnp.zeros((vt_ref.shape[2], tq), F32)
        for c in range(n_chunks):
            p = jnp.exp2(scores(c) - bound).astype(BF16)
            acc = acc + jnp.dot(vt_ref[0, c], p, preferred_element_type=F32)
        finish(acc)

    @pl.when(bound > SOFTMAX_SAFE_BOUND)
    def _():
        m = jnp.full((1, tq), NEG_INF, F32)
        acc = jnp.zeros((vt_ref.shape[2], tq), F32)
        for c in range(n_chunks):
            s = scores(c)
            m_new = jnp.maximum(m, jnp.max(s, axis=0, keepdims=True))
            alpha = jnp.exp2(m - m_new)
            p = jnp.exp2(s - m_new).astype(BF16)
            acc = alpha * acc + jnp.dot(vt_ref[0, c], p, preferred_element_type=F32)
            m = m_new
        finish(acc)


def _mla_attention(q, k, v_t, o_buf, o_off, bound):
    h, s, dq = q.shape
    t = k.shape[1]
    n_chunks, dva, tkc = v_t.shape[1:]
    dv = dva - FLASH_ONES_ROWS
    tq = _pick(s, FLASH_TQ, LANES)
    ob = o_off // dv
    return pl.pallas_call(
        functools.partial(_flash_body, n_chunks=n_chunks, tkc=tkc),
        grid=(h, s // tq),
        in_specs=[pl.BlockSpec(memory_space=pltpu.SMEM),
                  pl.BlockSpec((1, tq, dq), lambda hh, i: (hh, i, 0)),
                  pl.BlockSpec((1, t, dq), lambda hh, i: (hh, 0, 0)),
                  pl.BlockSpec((1, n_chunks, dva, tkc), lambda hh, i: (hh, 0, 0, 0)),
                  pl.BlockSpec(memory_space=pl.ANY)],
        out_specs=pl.BlockSpec((tq, dv), lambda hh, i: (i, ob + hh)),
        out_shape=jax.ShapeDtypeStruct(o_buf.shape, o_buf.dtype),
        input_output_aliases={4: 0},
        compiler_params=_params("parallel", "arbitrary"),
        name="mla_flash_attention",
    )(bound, q, k, v_t, o_buf)


def _na_variant_rows(rows, variant):
    r0 = (0, NA_Q_ROWS, rows - NA_Q_ROWS)[variant]
    ks = min(max(r0 - NA_WIN_H // 2, 0), rows - NA_K_ROWS)
    return r0, ks


def _na_bias_body(rpb_ref, o_ref, *, rows):
    h = pl.program_id(0)
    pair = 2 * GRID_W
    kc = lax.broadcasted_iota(jnp.int32, (GRID_W, pair), 0)
    lane = lax.broadcasted_iota(jnp.int32, (GRID_W, pair), 1)
    qc = jnp.where(lane < GRID_W, lane, lane - GRID_W)
    c_start = jnp.clip(qc - NA_WIN_W // 2, 0, GRID_W - NA_WIN_W)
    in_cols = (kc >= c_start) & (kc < c_start + NA_WIN_W)
    diag = kc - qc + (NA_WIN_W - 1)
    masked = jnp.full((GRID_W, pair), NEG_INF, F32)
    n_dr = 2 * NA_WIN_H - 1
    tiles = []
    for dr in range(n_dr):
        t = jnp.zeros((GRID_W, pair), F32)
        for jj in range(2 * NA_WIN_W - 1):
            t = jnp.where(diag == jj, rpb_ref[h, dr * (2 * NA_WIN_W - 1) + jj] * LOG2E, t)
        tiles.append(jnp.where(in_cols, t, masked))

    def row_tile(variant, a, b):
        r0, ks = _na_variant_rows(rows, variant)
        i, j = r0 + a, ks + b
        r_start = min(max(i - NA_WIN_H // 2, 0), rows - NA_WIN_H)
        if r_start <= j < r_start + NA_WIN_H:
            return tiles[j - i + NA_WIN_H - 1]
        return masked

    left = lane < GRID_W
    for variant in range(3):
        for b in range(NA_K_ROWS):
            for p in range(NA_Q_ROWS // 2):
                t0, t1 = row_tile(variant, 2 * p, b), row_tile(variant, 2 * p + 1, b)
                tile = t0 if t0 is t1 else jnp.where(left, t0, t1)
                o_ref[0, variant, b * GRID_W:(b + 1) * GRID_W, p * pair:(p + 1) * pair] = tile


def _na_bias_table(rpb, rows):
    h = rpb.shape[0]
    nq, nk = NA_Q_ROWS * GRID_W, NA_K_ROWS * GRID_W
    return pl.pallas_call(
        functools.partial(_na_bias_body, rows=rows),
        grid=(h,),
        in_specs=[pl.BlockSpec(memory_space=pltpu.SMEM)],
        out_specs=pl.BlockSpec((1, 3, nk, nq), lambda hh: (hh, 0, 0, 0)),
        out_shape=jax.ShapeDtypeStruct((h, 3, nk, nq), F32),
        compiler_params=_params("arbitrary"),
        name="na_bias_table",
    )(rpb.reshape(h, -1))


def _na_body(bound_ref, q_ref, k_ref, vt_ref, kc_ref, vct_ref, b_ref, _, o_ref, *, rows):
    rb = pl.program_id(1)
    ks = jnp.clip(rb * NA_Q_ROWS - NA_WIN_H // 2, 0, rows - NA_K_ROWS)
    start = pl.multiple_of(ks * GRID_W, NA_V_CHUNK)
    cs = lax.div(ks, NA_V_CHUNK // GRID_W)
    nk = NA_K_ROWS * GRID_W
    upper, width = bound_ref[0], bound_ref[1]

    def head(g, use_bound):
        cols = slice(g * HEAD_DIM, (g + 1) * HEAD_DIM)
        q = q_ref[:, cols]
        s_win = lax.dot_general(k_ref[pl.ds(start, nk), cols], q, _NT_DIMS, preferred_element_type=F32)
        s_win = s_win + b_ref[g, 0]
        s_ctx = lax.dot_general(kc_ref[:, cols], q, _NT_DIMS, preferred_element_type=F32)
        if use_bound:
            m = upper
        else:
            m = jnp.maximum(jnp.max(s_win, axis=0, keepdims=True), jnp.max(s_ctx, axis=0, keepdims=True))
        acc = jnp.dot(vct_ref[g], jnp.exp2(s_ctx - m).astype(BF16), preferred_element_type=F32)
        p_win = jnp.exp2(s_win - m).astype(BF16)
        for i in range(nk // NA_V_CHUNK):
            acc = acc + jnp.dot(vt_ref[g, cs + i], p_win[i * NA_V_CHUNK:(i + 1) * NA_V_CHUNK],
                                preferred_element_type=F32)
        out = acc[:HEAD_DIM] * (1.0 / acc[HEAD_DIM:HEAD_DIM + 1])
        o_ref[:, cols] = out.T.astype(o_ref.dtype)

    @pl.when(width <= 2.0 * SOFTMAX_SAFE_BOUND)
    def _():
        for g in range(NA_HEADS_PER_STEP):
            head(g, True)

    @pl.when(width > 2.0 * SOFTMAX_SAFE_BOUND)
    def _():
        for g in range(NA_HEADS_PER_STEP):
            head(g, False)


def _vt_body(v_ref, o_ref, *, chunk, per_step):
    dv = v_ref.shape[1]
    row = lax.broadcasted_iota(jnp.int32, (FLASH_ONES_ROWS, chunk), 0)
    ones_rows = jnp.where(row == 0, 1.0, 0.0).astype(o_ref.dtype)
    for i in range(per_step):
        x = v_ref[i * chunk:(i + 1) * chunk, :].astype(F32)
        o_ref[0, i, 0:dv, :] = x.T.astype(o_ref.dtype)
        o_ref[0, i, dv:, :] = ones_rows


def _values_transposed(src, off, stride, n_heads, dv, chunk):
    t = src.shape[0]
    n_chunks = t // chunk
    per_step = _pick(n_chunks, max(1, VT_ROWS_PER_STEP // chunk), 1)
    assert off % dv == 0 and stride % dv == 0
    cb, cs = off // dv, stride // dv
    return pl.pallas_call(
        functools.partial(_vt_body, chunk=chunk, per_step=per_step),
        grid=(n_heads, n_chunks // per_step),
        in_specs=[pl.BlockSpec((per_step * chunk, dv), lambda hh, r: (r, cb + cs * hh))],
        out_specs=pl.BlockSpec((1, per_step, dv + FLASH_ONES_ROWS, chunk), lambda hh, r: (hh, r, 0, 0)),
        out_shape=jax.ShapeDtypeStruct((n_heads, n_chunks, dv + FLASH_ONES_ROWS, chunk), src.dtype),
        compiler_params=_params("arbitrary", "arbitrary"),
        name="values_transposed",
    )(src)


def _na_attention(qk, v, ck, cv, bias, rows, o_buf, bound):
    s = qk.shape[0]
    c = ck.shape[0]
    nq, nk = NA_Q_ROWS * GRID_W, NA_K_ROWS * GRID_W
    n_blk = rows // NA_Q_ROWS
    n_vc = s // NA_V_CHUNK
    assert nk % NA_V_CHUNK == 0 and ((rows - NA_K_ROWS) * GRID_W) % NA_V_CHUNK == 0
    v_t = _values_transposed(v, 0, HEAD_DIM, NA_HEADS, HEAD_DIM, NA_V_CHUNK)
    vc_t = _values_transposed(cv, 0, HEAD_DIM, NA_HEADS, HEAD_DIM, c)[:, 0]
    dva = HEAD_DIM + FLASH_ONES_ROWS

    def variant(rb):
        return jnp.where(rb == 0, 0, jnp.where(rb == n_blk - 1, 2, 1))

    hps = NA_HEADS_PER_STEP
    gw = hps * HEAD_DIM
    n_grp = NA_HEADS // hps
    return pl.pallas_call(
        functools.partial(_na_body, rows=rows),
        grid=(n_grp, n_blk),
        in_specs=[pl.BlockSpec(memory_space=pltpu.SMEM),
                  pl.BlockSpec((nq, gw), lambda hg, rb: (rb, hg)),
                  pl.BlockSpec((s, gw), lambda hg, rb: (0, n_grp + hg)),
                  pl.BlockSpec((hps, n_vc, dva, NA_V_CHUNK), lambda hg, rb: (hg, 0, 0, 0)),
                  pl.BlockSpec((c, gw), lambda hg, rb: (0, hg)),
                  pl.BlockSpec((hps, dva, c), lambda hg, rb: (hg, 0, 0)),
                  pl.BlockSpec((hps, 1, nk, nq), lambda hg, rb: (hg, variant(rb), 0, 0)),
                  pl.BlockSpec(memory_space=pl.ANY)],
        out_specs=pl.BlockSpec((nq, gw), lambda hg, rb: (rb, hg)),
        out_shape=jax.ShapeDtypeStruct(o_buf.shape, o_buf.dtype),
        input_output_aliases={7: 0},
        compiler_params=_params("parallel", "arbitrary"),
        name="neighbourhood_attention",
    )(bound, qk, qk, v_t, ck, vc_t, bias, o_buf)


def _dwconv_body(prev_ref, cur_ref, next_ref, w_ref, bdw_ref, g_ref, b_ref, o_ref, win_ref, conv_ref,
                 *, n_blk):
    i = pl.program_id(0)
    tm, d = cur_ref.shape
    nc = d // LANES
    has_prev = i > 0
    has_next = i < n_blk - 1
    for c in range(nc):
        sl = slice(c * LANES, (c + 1) * LANES)
        win_ref[c, 0:CONV_HALO, :] = jnp.where(has_prev, prev_ref[:, sl].astype(F32), 0.0)
        win_ref[c, CONV_HALO:CONV_HALO + tm, :] = cur_ref[:, sl].astype(F32)
        win_ref[c, CONV_HALO + tm:, :] = jnp.where(has_next, next_ref[:, sl].astype(F32), 0.0)

    groups = (CONV_WIDTH + SUBLANES) // SUBLANES
    span = tm + SUBLANES * (groups - 1)

    def chunk(c, carry):
        w = w_ref[c]
        acc = jnp.zeros((tm, LANES), F32) + bdw_ref[c]
        for b in range(SUBLANES):
            shifted = win_ref[c, pl.ds(b, span), :]
            for a in range(groups):
                o = SUBLANES * a + b
                if 1 <= o <= CONV_WIDTH:
                    acc = acc + shifted[SUBLANES * a:SUBLANES * a + tm, :] * w[o - 1:o, :]
        conv_ref[c] = acc
        return carry

    lax.fori_loop(0, nc, chunk, 0)

    total = conv_ref[0]
    for c in range(1, nc):
        total = total + conv_ref[c]
    mu = jnp.sum(total, axis=-1, keepdims=True) * (1.0 / d)
    sq = jnp.zeros((tm, LANES), F32)
    for c in range(nc):
        dlt = conv_ref[c] - mu
        sq = sq + dlt * dlt
    rstd = lax.rsqrt(jnp.sum(sq, axis=-1, keepdims=True) * (1.0 / d) + LN_EPS)
    for c in range(nc):
        y = (conv_ref[c] - mu) * rstd * g_ref[c] + b_ref[c]
        o_ref[:, c * LANES:(c + 1) * LANES] = (y * jax.nn.sigmoid(y)).astype(o_ref.dtype)


def _dwconv_ln_silu(u, w_dw, b_dw, g_ln, b_ln):
    m, d = u.shape
    tm = _pick(m, 128, CONV_HALO)
    nc = d // LANES
    n_blk = m // tm
    per = tm // CONV_HALO
    last_halo = m // CONV_HALO - 1
    assert CONV_WIDTH // 2 < CONV_HALO and CONV_WIDTH + 1 <= 2 * CONV_HALO
    vec = lambda v: v.reshape(nc, 1, LANES).astype(F32)
    w_chunks = jnp.transpose(w_dw.astype(F32).reshape(CONV_WIDTH, nc, LANES), (1, 0, 2))
    whole = lambda shape: pl.BlockSpec(shape, lambda i: (0,) * len(shape))
    return pl.pallas_call(
        functools.partial(_dwconv_body, n_blk=n_blk),
        grid=(n_blk,),
        in_specs=[pl.BlockSpec((CONV_HALO, d), lambda i: (jnp.maximum(i * per - 1, 0), 0)),
                  pl.BlockSpec((tm, d), lambda i: (i, 0)),
                  pl.BlockSpec((CONV_HALO, d), lambda i: (jnp.minimum((i + 1) * per, last_halo), 0)),
                  whole((nc, CONV_WIDTH, LANES)), whole((nc, 1, LANES)), whole((nc, 1, LANES)),
                  whole((nc, 1, LANES))],
        out_specs=pl.BlockSpec((tm, d), lambda i: (i, 0)),
        out_shape=jax.ShapeDtypeStruct((m, d), BF16),
        scratch_shapes=[pltpu.VMEM((nc, tm + 2 * CONV_HALO, LANES), F32),
                        pltpu.VMEM((nc, tm, LANES), F32)],
        compiler_params=_params("arbitrary"),
        name="dwconv_ln_silu",
    )(u, u, u, w_chunks, vec(b_dw), vec(g_ln), vec(b_ln))


def _rope_tables(n_ctx, n_tok):
    t = jnp.arange(n_tok, dtype=jnp.int32)
    pos = jnp.stack([t // GRID_W, t % GRID_W], axis=-1).astype(F32)
    n_freq = MLA_ROPE // 4
    inv_freq = ROPE_THETA ** (-jnp.arange(n_freq, dtype=F32) / n_freq)
    ang = (pos[:, :, None] * inv_freq).reshape(n_tok, 2 * n_freq)
    cos, sin = jnp.cos(ang), jnp.sin(ang)
    zeros = jnp.zeros((n_tok, LANES - MLA_ROPE), F32)
    cos_lat = jnp.concatenate([cos, cos, zeros], axis=-1)
    sin_lat = jnp.concatenate([-sin, sin, zeros], axis=-1)
    cos_ctx = jnp.concatenate([jnp.ones((n_ctx, MLA_ROPE), F32), jnp.zeros((n_ctx, LANES - MLA_ROPE), F32)], -1)
    sin_ctx = jnp.zeros((n_ctx, LANES), F32)
    return (jnp.concatenate([cos_ctx, cos_lat], 0), jnp.concatenate([sin_ctx, sin_lat], 0))


def _attention_layer(x, xc, mod, layer, j, p):
    s, d = x.shape
    c = xc.shape[0]
    rows = s // GRID_W
    assert s % GRID_W == 0 and rows % NA_Q_ROWS == 0 and rows >= NA_K_ROWS
    w_in = functools.partial(_matmul, w=jnp.swapaxes(p["att_w_in"], 1, 2), w_layer=j, w_transposed=True)

    hl = _rms_norm(x, p["g_mix"][layer], mod, layer, 0, 0, 1)
    hc = _rms_norm(xc, p["g_mix"][layer], mod, layer, 1, 0, 1)

    g_q = jnp.tile(p["att_g_na_q"][j], NA_HEADS) * (HEAD_DIM ** -0.5 * LOG2E)
    g_k = jnp.tile(p["att_g_na_k"][j], NA_HEADS)

    qk = w_in(hl, off=0, n=2 * NA_WIDTH, kind="headnorm", gain=jnp.concatenate([g_q, g_k]))
    na_v = w_in(hl, off=OFF_NA_V, n=NA_WIDTH)
    q_lat = w_in(hl, off=OFF_Q_LAT, n=MLA_Q_RANK)
    kv_lat = w_in(hl, off=OFF_KV_LAT, n=MLA_KV_RANK)
    k_rope = w_in(hl, off=OFF_K_ROPE, n=MLA_ROPE, kind="pad_lanes")
    c_k = w_in(hc, off=OFF_NA_K, n=NA_WIDTH, kind="headnorm", gain=g_k)
    c_v = w_in(hc, off=OFF_NA_V, n=NA_WIDTH)
    c_kv_lat = w_in(hc, off=OFF_KV_LAT, n=MLA_KV_RANK)
    c_k_rope = w_in(hc, off=OFF_K_ROPE, n=MLA_ROPE, kind="pad_lanes")

    w_qb = p["att_w_qb"][j].reshape(MLA_Q_RANK, MLA_HEADS, MLA_QK)
    w_qb_rope = jnp.pad(w_qb[:, :, MLA_NOPE:], ((0, 0), (0, 0), (0, LANES - MLA_ROPE)))
    w_qb_perm = jnp.concatenate([w_qb[:, :, :MLA_NOPE].reshape(MLA_Q_RANK, -1),
                                 w_qb_rope.reshape(MLA_Q_RANK, -1)], axis=-1)

    q_raw = _matmul(_rms_norm(q_lat, p["att_g_qa"][j]), w_qb_perm, tn_pref=1024)
    kv_n = _rms_norm(jnp.concatenate([c_kv_lat, kv_lat], axis=0), p["att_g_kva"][j])
    kv_raw = _matmul(kv_n, p["att_w_kvb"], w_layer=j, tn_pref=4096)
    kv_head = MLA_NOPE + MLA_V
    kr_all = jnp.concatenate([c_k_rope, k_rope], axis=0)

    cos, sin = _rope_tables(c, s)
    q_full = _mla_heads(q_raw, 0, MLA_NOPE, q_raw, MLA_HEADS * MLA_NOPE, False, p["att_g_mla_q"][j],
                        cos[c:], sin[c:], MLA_QK ** -0.5 * LOG2E)
    k_full = _mla_heads(kv_raw, 0, kv_head, kr_all, 0, True, p["att_g_mla_k"][j], cos, sin, 1.0)
    tkc = _pick(c + s, FLASH_TK, LANES)
    v_t = _values_transposed(kv_raw, MLA_NOPE, kv_head, MLA_HEADS, MLA_V, tkc)

    bias = _na_bias_table(p["att_na_rpb"][j], rows)
    o = jnp.zeros((s, NA_WIDTH + MLA_HEADS * MLA_V), BF16)
    rpb_log2 = p["att_na_rpb"][j] * LOG2E
    qk_cap = (HEAD_DIM * HEAD_DIM ** -0.5 * LOG2E * BF16_NORM_SLACK) * (
        jnp.max(jnp.abs(p["att_g_na_q"][j])) * jnp.max(jnp.abs(p["att_g_na_k"][j])))
    rpb_hi = jnp.maximum(jnp.max(rpb_log2), 0.0)
    rpb_lo = jnp.minimum(jnp.min(rpb_log2), 0.0)
    na_bound = jnp.stack([qk_cap + rpb_hi, 2.0 * qk_cap + rpb_hi - rpb_lo]).astype(F32)
    o = _na_attention(qk, na_v, c_k, c_v, bias, rows, o, na_bound)
    bound = (MLA_QK * MLA_QK ** -0.5 * LOG2E * BF16_NORM_SLACK) * (
        jnp.max(jnp.abs(p["att_g_mla_q"][j])) * jnp.max(jnp.abs(p["att_g_mla_k"][j])))
    o = _mla_attention(q_full, k_full, v_t, o, NA_WIDTH, bound.reshape(1).astype(F32))
    return _matmul(o, p["att_w_out"], w_layer=j, kind="resid", out_dtype=F32, res=x, gate=mod,
                   gate_layer=layer, gate_row=0, gate_chunk=2)


def _conv_layer(x, mod, layer, j, p):
    d = x.shape[1]
    hl = _rms_norm(x, p["g_mix"][layer], mod, layer, 0, 0, 1)
    b1 = p["conv_b_pw1"][j]
    u = _matmul(hl, p["conv_w_pw1"], w_layer=j, off=0, off2=d, n=d, kind="glu", bias=b1[:d],
                bias2=b1[d:], tn_pref=256)
    v = _dwconv_ln_silu(u, p["conv_w_dw"][j], p["conv_b_dw"][j], p["conv_g_ln"][j], p["conv_b_ln"][j])
    return _matmul(v, p["conv_w_pw2"], w_layer=j, kind="resid", out_dtype=F32, bias=p["conv_b_pw2"][j],
                   res=x, gate=mod, gate_layer=layer, gate_row=0, gate_chunk=2)


def _mlp(x, mod, layer, p):
    h = _rms_norm(x, p["g_mlp"][layer], mod, layer, 0, 3, 4)
    u = _matmul(h, p["mlp_w1"], w_layer=layer, kind="relu2")
    return _matmul(u, p["mlp_w2"], w_layer=layer, kind="resid", out_dtype=F32, res=x, gate=mod,
                   gate_layer=layer, gate_row=0, gate_chunk=5, tm_pref=2048, tn_pref=1024, tk_pref=1024)


def kernel(x, c, ctx, c_ctx, ada_w, ada_b, g_mix, g_mlp, mlp_w1, mlp_w2, att_w_in, att_g_qa, att_w_qb, att_g_kva, att_w_kvb, att_g_na_q, att_g_na_k, att_na_rpb, att_g_mla_q, att_g_mla_k, att_w_out, conv_w_pw1, conv_b_pw1, conv_w_dw, conv_b_dw, conv_g_ln, conv_b_ln, conv_w_pw2, conv_b_pw2):
    p = dict(g_mix=g_mix, g_mlp=g_mlp, mlp_w1=mlp_w1, mlp_w2=mlp_w2, att_w_in=att_w_in,
             att_g_qa=att_g_qa, att_w_qb=att_w_qb, att_g_kva=att_g_kva, att_w_kvb=att_w_kvb,
             att_g_na_q=att_g_na_q, att_g_na_k=att_g_na_k, att_na_rpb=att_na_rpb,
             att_g_mla_q=att_g_mla_q, att_g_mla_k=att_g_mla_k, att_w_out=att_w_out,
             conv_w_pw1=conv_w_pw1, conv_b_pw1=conv_b_pw1, conv_w_dw=conv_w_dw, conv_b_dw=conv_b_dw,
             conv_g_ln=conv_g_ln, conv_b_ln=conv_b_ln, conv_w_pw2=conv_w_pw2, conv_b_pw2=conv_b_pw2)
    batch, _, d = x.shape
    depth = ada_w.shape[0]
    assert batch == 1 and c.shape[0] == 1
    svec = jnp.zeros((MOD_ROWS, d), F32).at[0].set(c[0]).at[1].set(c_ctx)
    mod = _modulation(svec, ada_w, ada_b)

    xl = x[0]
    xc = ctx[0]
    for layer in range(depth):
        j = layer // 2
        if layer % 2 == 0:
            assert not any(m % 2 == 0 for m in range(layer + 1, depth))
            xl = _attention_layer(xl, xc, mod, layer, j, p)
        else:
            xl = _conv_layer(xl, mod, layer, j, p)
        xl = _mlp(xl, mod, layer, p)
    return xl[None]
```

```python
import functools
import math

import jax
import jax.numpy as jnp
from jax import lax
from jax.experimental import pallas as pl
from jax.experimental.pallas import tpu as pltpu

F32 = jnp.float32
BF16 = jnp.bfloat16

GRID_W = 64
NA_HEADS = 16
HEAD_DIM = 128
NA_WIN_H = 8
NA_WIN_W = 16
NA_WIDTH = NA_HEADS * HEAD_DIM
MLA_HEADS = 16
MLA_Q_RANK = 1536
MLA_KV_RANK = 512
MLA_NOPE = 128
MLA_ROPE = 64
MLA_V = 128
MLA_QK = MLA_NOPE + MLA_ROPE
OFF_NA_K = NA_WIDTH
OFF_NA_V = 2 * NA_WIDTH
OFF_Q_LAT = 3 * NA_WIDTH
OFF_KV_LAT = OFF_Q_LAT + MLA_Q_RANK
OFF_K_ROPE = OFF_KV_LAT + MLA_KV_RANK
CONV_WIDTH = 31
ROPE_THETA = 10000.0
NORM_EPS = 1e-6
LN_EPS = 1e-5

LANES = 128
SUBLANES = 8
V7X_VMEM_LIMIT = 56 * 1024 * 1024

MOD_ROWS = 8
NA_Q_ROWS = 8
NA_K_ROWS = NA_Q_ROWS + NA_WIN_H
NA_V_CHUNK = 256
NA_HEADS_PER_STEP = 2
VT_ROWS_PER_STEP = 2048
FLASH_TQ = 1024
FLASH_TK = 2816
FLASH_ONES_ROWS = 16
CONV_HALO = 16

NEG_INF = float("-inf")
LOG2E = math.log2(math.e)
_NT_DIMS = (((1,), (1,)), ((), ()))
SOFTMAX_SAFE_BOUND = 60.0
BF16_NORM_SLACK = 1.01


def _pick(dim, pref, align):
    if dim <= pref:
        return dim
    for t in range(pref - pref % align, 0, -align):
        if dim % t == 0:
            return t
    raise ValueError(f"no tile for {dim}")


def _params(*sem):
    return pltpu.CompilerParams(dimension_semantics=sem, vmem_limit_bytes=V7X_VMEM_LIMIT)


def _mod_body(s_ref, w_ref, b_ref, o_ref):
    s = s_ref[...]
    s = s * jax.nn.sigmoid(s)
    o_ref[0] = jnp.dot(s.astype(BF16), w_ref[0].astype(BF16), preferred_element_type=F32) + b_ref[0]


def _modulation(svec, ada_w, ada_b):
    depth, d, n = ada_w.shape
    tn = _pick(n, 512, LANES)
    return pl.pallas_call(
        _mod_body,
        grid=(depth, n // tn),
        in_specs=[pl.BlockSpec((MOD_ROWS, d), lambda l, j: (0, 0)),
                  pl.BlockSpec((1, d, tn), lambda l, j: (l, 0, j)),
                  pl.BlockSpec((1, 1, tn), lambda l, j: (l, 0, j))],
        out_specs=pl.BlockSpec((1, MOD_ROWS, tn), lambda l, j: (l, 0, j)),
        out_shape=jax.ShapeDtypeStruct((depth, MOD_ROWS, n), F32),
        compiler_params=_params("arbitrary", "arbitrary"),
        name="modulation",
    )(svec, ada_w, ada_b.reshape(depth, 1, n))


def _norm_body(*refs, row, modulated):
    if modulated:
        x_ref, g_ref, shift_ref, scale_ref, o_ref = refs
    else:
        x_ref, g_ref, o_ref = refs
    x = x_ref[...].astype(F32)
    ms = jnp.mean(x * x, axis=-1, keepdims=True)
    y = x * lax.rsqrt(ms + NORM_EPS) * g_ref[...]
    if modulated:
        y = y * (1.0 + scale_ref[0, row:row + 1, :]) + shift_ref[0, row:row + 1, :]
    o_ref[...] = y.astype(o_ref.dtype)


def _rms_norm(x, g, mod=None, layer=0, row=0, shift_chunk=0, scale_chunk=1):
    m, d = x.shape
    tm = _pick(m, 512, 16)
    in_specs = [pl.BlockSpec((tm, d), lambda i: (i, 0)), pl.BlockSpec((1, d), lambda i: (0, 0))]
    args = [x, g.reshape(1, d)]
    if mod is not None:
        in_specs += [pl.BlockSpec((1, MOD_ROWS, d), lambda i: (layer, 0, shift_chunk)),
                     pl.BlockSpec((1, MOD_ROWS, d), lambda i: (layer, 0, scale_chunk))]
        args += [mod, mod]
    return pl.pallas_call(
        functools.partial(_norm_body, row=row, modulated=mod is not None),
        grid=(m // tm,),
        in_specs=in_specs,
        out_specs=pl.BlockSpec((tm, d), lambda i: (i, 0)),
        out_shape=jax.ShapeDtypeStruct((m, d), BF16),
        compiler_params=_params("arbitrary"),
        name="rms_norm",
    )(*args)


def _head_norm(acc, g):
    outs = []
    for c in range(acc.shape[1] // HEAD_DIM):
        blk = acc[:, c * HEAD_DIM:(c + 1) * HEAD_DIM]
        ms = jnp.mean(blk * blk, axis=-1, keepdims=True)
        outs.append(blk * lax.rsqrt(ms + NORM_EPS) * g[:, c * HEAD_DIM:(c + 1) * HEAD_DIM])
    return outs[0] if len(outs) == 1 else jnp.concatenate(outs, axis=-1)


def _epilogue(kind, acc, acc2, ex, gate_row):
    if kind == "plain":
        return acc
    if kind == "relu2":
        r = jnp.maximum(acc, 0.0)
        return r * r
    if kind == "headnorm":
        return _head_norm(acc, ex["gain"][...])
    if kind == "glu":
        b = ex["bias"][...]
        b2 = ex["bias2"][...]
        return (acc + b) * jax.nn.sigmoid(acc2 + b2)
    if kind == "resid":
        y = acc
        if "bias" in ex:
            y = y + ex["bias"][...]
        return ex["res"][...] + ex["gate"][0, gate_row:gate_row + 1, :] * y
    raise ValueError(kind)


def _mm_body(*refs, kind, names, dual, k_steps, gate_row, w_transposed):
    a_ref, w_ref = refs[0], refs[1]
    pos = 2
    w2_ref = None
    if dual:
        w2_ref = refs[pos]
        pos += 1
    ex = dict(zip(names, refs[pos:pos + len(names)]))
    pos += len(names)
    o_ref = refs[pos]
    scratch = refs[pos + 1:]

    def mm(lhs, w_tile_ref):
        w_tile = w_tile_ref[...].astype(BF16)
        if w_transposed:
            return lax.dot_general(lhs, w_tile, _NT_DIMS, preferred_element_type=F32)
        return jnp.dot(lhs, w_tile, preferred_element_type=F32)

    a = a_ref[...]
    if k_steps == 1:
        part = mm(a, w_ref)
        if kind == "pad_lanes":
            n = part.shape[1]
            o_ref[:, :n] = part.astype(o_ref.dtype)
            o_ref[:, n:] = jnp.zeros((part.shape[0], o_ref.shape[1] - n), o_ref.dtype)
            return
        part2 = mm(a, w2_ref) if dual else None
        o_ref[...] = _epilogue(kind, part, part2, ex, gate_row).astype(o_ref.dtype)
        return

    acc_ref = scratch[0] if scratch else o_ref
    k = pl.program_id(2)

    @pl.when(k == 0)
    def _():
        acc_ref[...] = jnp.zeros(acc_ref.shape, F32)

    acc_ref[...] += mm(a, w_ref)

    @pl.when(k == k_steps - 1)
    def _():
        o_ref[...] = _epilogue(kind, acc_ref[...], None, ex, gate_row).astype(o_ref.dtype)


def _matmul(a, w, *, off=0, n=None, kind="plain", out_dtype=BF16, off2=None,
            gain=None, bias=None, bias2=None, res=None, gate=None, gate_layer=0, gate_row=0,
            gate_chunk=0, w_layer=None, w_transposed=False, tm_pref=1024, tn_pref=512, tk_pref=4096):
    m, kdim = a.shape
    n = w.shape[-2 if w_transposed else -1] if n is None else n
    tm = _pick(m, tm_pref, 16)
    tn = _pick(n, tn_pref, LANES)
    tk = _pick(kdim, tk_pref, LANES)
    assert off % tn == 0 and (off2 is None or off2 % tn == 0)
    k_steps = kdim // tk
    dual = off2 is not None
    assert not (dual and k_steps > 1)
    ob, ob2 = off // tn, (off2 // tn if dual else 0)
    pad_lanes = kind == "pad_lanes"
    assert not pad_lanes or (n == tn < LANES and k_steps == 1)
    n_out, tn_out = (LANES, LANES) if pad_lanes else (n, tn)

    def w_spec(col0):
        if w_transposed:
            block, index = (tn, tk), lambda i, j, k: (col0 + j, k)
        else:
            block, index = (tk, tn), lambda i, j, k: (k, col0 + j)
        if w_layer is None:
            return pl.BlockSpec(block, index)
        return pl.BlockSpec((None,) + block, lambda i, j, k: (w_layer,) + index(i, j, k))

    in_specs = [pl.BlockSpec((tm, tk), lambda i, j, k: (i, k)), w_spec(ob)]
    args = [a, w]
    if dual:
        in_specs.append(w_spec(ob2))
        args.append(w)
    names = []
    for name, vec in (("gain", gain), ("bias", bias), ("bias2", bias2)):
        if vec is not None:
            names.append(name)
            in_specs.append(pl.BlockSpec((1, tn), lambda i, j, k: (0, j)))
            args.append(vec.reshape(1, n).astype(F32))
    if res is not None:
        names.append("res")
        in_specs.append(pl.BlockSpec((tm, tn), lambda i, j, k: (i, j)))
        args.append(res)
    if gate is not None:
        names.append("gate")
        gb = gate_chunk * (n // tn)
        in_specs.append(pl.BlockSpec((1, MOD_ROWS, tn), lambda i, j, k: (gate_layer, 0, gb + j)))
        args.append(gate)

    scratch = [pltpu.VMEM((tm, tn), F32)] if k_steps > 1 and out_dtype != F32 else []
    return pl.pallas_call(
        functools.partial(_mm_body, kind=kind, names=tuple(names), dual=dual, k_steps=k_steps,
                          gate_row=gate_row, w_transposed=w_transposed),
        grid=(m // tm, n // tn, k_steps),
        in_specs=in_specs,
        out_specs=pl.BlockSpec((tm, tn_out), lambda i, j, k: (i, j)),
        out_shape=jax.ShapeDtypeStruct((m, n_out), out_dtype),
        scratch_shapes=scratch,
        compiler_params=_params("parallel", "parallel", "arbitrary"),
        name="matmul_" + kind,
    )(*args)


def _mla_head_body(nope_ref, rope_ref, gn_ref, gr_ref, cos_ref, sin_ref, o_ref, *, shared_rope,
                   out_scale, nope_stride):
    lane = lax.broadcasted_iota(jnp.int32, (1, LANES), 1)
    first_half = lane < MLA_ROPE // 2
    cos = cos_ref[...]
    sin = sin_ref[...]
    gn = gn_ref[...]
    gr = gr_ref[...]
    ones = jnp.ones((LANES, LANES), BF16)
    for h in range(MLA_HEADS):
        nope = nope_ref[:, h * nope_stride:h * nope_stride + MLA_NOPE].astype(F32)
        if shared_rope:
            rope = rope_ref[...].astype(F32)
        else:
            rope = rope_ref[:, h * LANES:(h + 1) * LANES].astype(F32)
        sq = nope * nope + rope * rope
        hi = sq.astype(BF16)
        lo = (sq - hi.astype(F32)).astype(BF16)
        ss = (jnp.dot(hi, ones, preferred_element_type=F32)
              + jnp.dot(lo, ones, preferred_element_type=F32))
        inv = lax.rsqrt(ss * (1.0 / MLA_QK) + NORM_EPS)
        nope = nope * inv * gn
        rope = rope * inv * gr
        partner = jnp.where(first_half, pltpu.roll(rope, LANES - MLA_ROPE // 2, 1),
                            pltpu.roll(rope, MLA_ROPE // 2, 1))
        rope = rope * cos + partner * sin
        o_ref[h, :, 0:MLA_NOPE] = (nope * out_scale).astype(o_ref.dtype)
        o_ref[h, :, MLA_NOPE:] = (rope * out_scale).astype(o_ref.dtype)


def _mla_heads(nope_src, nope_off, nope_stride, rope_src, rope_off, shared_rope, g, cos, sin, out_scale):
    m = nope_src.shape[0]
    tm = _pick(m, 256, 16)
    g_nope = g[:MLA_NOPE].reshape(1, MLA_NOPE).astype(F32)
    g_rope = jnp.pad(g[MLA_NOPE:], (0, LANES - MLA_ROPE)).reshape(1, LANES).astype(F32)
    nw = MLA_HEADS * nope_stride
    nb = nope_off // nw
    if shared_rope:
        rope_spec = pl.BlockSpec((tm, LANES), lambda i: (i, 0))
    else:
        rw = MLA_HEADS * LANES
        rb = rope_off // rw
        rope_spec = pl.BlockSpec((tm, rw), lambda i: (i, rb))
    return pl.pallas_call(
        functools.partial(_mla_head_body, shared_rope=shared_rope, out_scale=out_scale,
                          nope_stride=nope_stride),
        grid=(m // tm,),
        in_specs=[pl.BlockSpec((tm, nw), lambda i: (i, nb)), rope_spec,
                  pl.BlockSpec((1, MLA_NOPE), lambda i: (0, 0)),
                  pl.BlockSpec((1, LANES), lambda i: (0, 0)),
                  pl.BlockSpec((tm, LANES), lambda i: (i, 0)),
                  pl.BlockSpec((tm, LANES), lambda i: (i, 0))],
        out_specs=pl.BlockSpec((MLA_HEADS, tm, 2 * LANES), lambda i: (0, i, 0)),
        out_shape=jax.ShapeDtypeStruct((MLA_HEADS, m, 2 * LANES), BF16),
        compiler_params=_params("arbitrary"),
        name="mla_head_norm_rope",
    )(nope_src, rope_src, g_nope, g_rope, cos, sin)


def _flash_body(bound_ref, q_ref, k_ref, vt_ref, _, o_ref, *, n_chunks, tkc):
    q = q_ref[0]
    tq = q.shape[0]
    bound = bound_ref[0]

    def scores(c):
        k_c = k_ref[0, c * tkc:(c + 1) * tkc, :]
        return lax.dot_general(k_c, q, _NT_DIMS, preferred_element_type=F32)

    def finish(acc):
        out = acc[:MLA_V] * (1.0 / acc[MLA_V:MLA_V + 1])
        o_ref[...] = out.T.astype(o_ref.dtype)

    bound_is_safe = bound <= SOFTMAX_SAFE_BOUND

    @pl.when(bound_is_safe)
    def _():
        acc = jnp.zeros((vt_ref.shape[2], tq), F32)
        for c in range(n_chunks):
            p = jnp.exp2(scores(c) - bound).astype(BF16)
            acc = acc + jnp.dot(vt_ref[0, c], p, preferred_element_type=F32)
        finish(acc)

    @pl.when(jnp.logical_not(bound_is_safe))
    def _():
        m = jnp.full((1, tq), NEG_INF, F32)
        acc = jnp.zeros((vt_ref.shape[2], tq), F32)
        for c in range(n_chunks):
            s = scores(c)
            m_new = jnp.maximum(m, jnp.max(s, axis=0, keepdims=True))
            alpha = jnp.exp2(m - m_new)
            p = jnp.exp2(s - m_new).astype(BF16)
            acc = alpha * acc + jnp.dot(vt_ref[0, c], p, preferred_element_type=F32)
            m = m_new
        finish(acc)


def _mla_attention(q, k, v_t, o_buf, o_off, bound):
    h, s, dq = q.shape
    t = k.shape[1]
    n_chunks, dva, tkc = v_t.shape[1:]
    dv = dva - FLASH_ONES_ROWS
    tq = _pick(s, FLASH_TQ, LANES)
    ob = o_off // dv
    return pl.pallas_call(
        functools.partial(_flash_body, n_chunks=n_chunks, tkc=tkc),
        grid=(h, s // tq),
        in_specs=[pl.BlockSpec(memory_space=pltpu.SMEM),
                  pl.BlockSpec((1, tq, dq), lambda hh, i: (hh, i, 0)),
                  pl.BlockSpec((1, t, dq), lambda hh, i: (hh, 0, 0)),
                  pl.BlockSpec((1, n_chunks, dva, tkc), lambda hh, i: (hh, 0, 0, 0)),
                  pl.BlockSpec(memory_space=pl.ANY)],
        out_specs=pl.BlockSpec((tq, dv), lambda hh, i: (i, ob + hh)),
        out_shape=jax.ShapeDtypeStruct(o_buf.shape, o_buf.dtype),
        input_output_aliases={4: 0},
        compiler_params=_params("parallel", "arbitrary"),
        name="mla_flash_attention",
    )(bound, q, k, v_t, o_buf)


def _na_variant_rows(rows, variant):
    r0 = (0, NA_Q_ROWS, rows - NA_Q_ROWS)[variant]
    ks = min(max(r0 - NA_WIN_H // 2, 0), rows - NA_K_ROWS)
    return r0, ks


def _na_bias_body(rpb_ref, o_ref, *, rows):
    h = pl.program_id(0)
    pair = 2 * GRID_W
    kc = lax.broadcasted_iota(jnp.int32, (GRID_W, pair), 0)
    lane = lax.broadcasted_iota(jnp.int32, (GRID_W, pair), 1)
    qc = jnp.where(lane < GRID_W, lane, lane - GRID_W)
    c_start = jnp.clip(qc - NA_WIN_W // 2, 0, GRID_W - NA_WIN_W)
    in_cols = (kc >= c_start) & (kc < c_start + NA_WIN_W)
    diag = kc - qc + (NA_WIN_W - 1)
    masked = jnp.full((GRID_W, pair), NEG_INF, F32)
    n_dr = 2 * NA_WIN_H - 1
    tiles = []
    for dr in range(n_dr):
        t = jnp.zeros((GRID_W, pair), F32)
        for jj in range(2 * NA_WIN_W - 1):
            t = jnp.where(diag == jj, rpb_ref[h, dr * (2 * NA_WIN_W - 1) + jj] * LOG2E, t)
        tiles.append(jnp.where(in_cols, t, masked))

    def row_tile(variant, a, b):
        r0, ks = _na_variant_rows(rows, variant)
        i, j = r0 + a, ks + b
        r_start = min(max(i - NA_WIN_H // 2, 0), rows - NA_WIN_H)
        if r_start <= j < r_start + NA_WIN_H:
            return tiles[j - i + NA_WIN_H - 1]
        return masked

    left = lane < GRID_W
    for variant in range(3):
        for b in range(NA_K_ROWS):
            for p in range(NA_Q_ROWS // 2):
                t0, t1 = row_tile(variant, 2 * p, b), row_tile(variant, 2 * p + 1, b)
                tile = t0 if t0 is t1 else jnp.where(left, t0, t1)
                o_ref[0, variant, b * GRID_W:(b + 1) * GRID_W, p * pair:(p + 1) * pair] = tile


def _na_bias_table(rpb, rows):
    h = rpb.shape[0]
    nq, nk = NA_Q_ROWS * GRID_W, NA_K_ROWS * GRID_W
    return pl.pallas_call(
        functools.partial(_na_bias_body, rows=rows),
        grid=(h,),
        in_specs=[pl.BlockSpec(memory_space=pltpu.SMEM)],
        out_specs=pl.BlockSpec((1, 3, nk, nq), lambda hh: (hh, 0, 0, 0)),
        out_shape=jax.ShapeDtypeStruct((h, 3, nk, nq), F32),
        compiler_params=_params("arbitrary"),
        name="na_bias_table",
    )(rpb.reshape(h, -1))


def _na_body(bound_ref, q_ref, k_ref, vt_ref, kc_ref, vct_ref, b_ref, _, o_ref, *, rows):
    rb = pl.program_id(1)
    ks = jnp.clip(rb * NA_Q_ROWS - NA_WIN_H // 2, 0, rows - NA_K_ROWS)
    start = pl.multiple_of(ks * GRID_W, NA_V_CHUNK)
    cs = lax.div(ks, NA_V_CHUNK // GRID_W)
    nk = NA_K_ROWS * GRID_W
    upper, width = bound_ref[0], bound_ref[1]

    def head(g, use_bound):
        cols = slice(g * HEAD_DIM, (g + 1) * HEAD_DIM)
        q = q_ref[:, cols]
        s_win = lax.dot_general(k_ref[pl.ds(start, nk), cols], q, _NT_DIMS, preferred_element_type=F32)
        s_win = s_win + b_ref[g, 0]
        s_ctx = lax.dot_general(kc_ref[:, cols], q, _NT_DIMS, preferred_element_type=F32)
        if use_bound:
            m = upper
        else:
            m = jnp.maximum(jnp.max(s_win, axis=0, keepdims=True), jnp.max(s_ctx, axis=0, keepdims=True))
        acc = jnp.dot(vct_ref[g], jnp.exp2(s_ctx - m).astype(BF16), preferred_element_type=F32)
        p_win = jnp.exp2(s_win - m).astype(BF16)
        for i in range(nk // NA_V_CHUNK):
            acc = acc + jnp.dot(vt_ref[g, cs + i], p_win[i * NA_V_CHUNK:(i + 1) * NA_V_CHUNK],
                                preferred_element_type=F32)
        out = acc[:HEAD_DIM] * (1.0 / acc[HEAD_DIM:HEAD_DIM + 1])
        o_ref[:, cols] = out.T.astype(o_ref.dtype)

    width_is_safe = width <= 2.0 * SOFTMAX_SAFE_BOUND

    @pl.when(width_is_safe)
    def _():
        for g in range(NA_HEADS_PER_STEP):
            head(g, True)

    @pl.when(jnp.logical_not(width_is_safe))
    def _():
        for g in range(NA_HEADS_PER_STEP):
            head(g, False)


def _vt_body(v_ref, o_ref, *, chunk, per_step):
    dv = v_ref.shape[1]
    row = lax.broadcasted_iota(jnp.int32, (FLASH_ONES_ROWS, chunk), 0)
    ones_rows = jnp.where(row == 0, 1.0, 0.0).astype(o_ref.dtype)
    for i in range(per_step):
        x = v_ref[i * chunk:(i + 1) * chunk, :].astype(F32)
        o_ref[0, i, 0:dv, :] = x.T.astype(o_ref.dtype)
        o_ref[0, i, dv:, :] = ones_rows


def _values_transposed(src, off, stride, n_heads, dv, chunk):
    t = src.shape[0]
    n_chunks = t // chunk
    per_step = _pick(n_chunks, max(1, VT_ROWS_PER_STEP // chunk), 1)
    assert off % dv == 0 and stride % dv == 0
    cb, cs = off // dv, stride // dv
    return pl.pallas_call(
        functools.partial(_vt_body, chunk=chunk, per_step=per_step),
        grid=(n_heads, n_chunks // per_step),
        in_specs=[pl.BlockSpec((per_step * chunk, dv), lambda hh, r: (r, cb + cs * hh))],
        out_specs=pl.BlockSpec((1, per_step, dv + FLASH_ONES_ROWS, chunk), lambda hh, r: (hh, r, 0, 0)),
        out_shape=jax.ShapeDtypeStruct((n_heads, n_chunks, dv + FLASH_ONES_ROWS, chunk), src.dtype),
        compiler_params=_params("arbitrary", "arbitrary"),
        name="values_transposed",
    )(src)


def _na_attention(qk, v, ck, cv, bias, rows, o_buf, bound):
    s = qk.shape[0]
    c = ck.shape[0]
    nq, nk = NA_Q_ROWS * GRID_W, NA_K_ROWS * GRID_W
    n_blk = rows // NA_Q_ROWS
    n_vc = s // NA_V_CHUNK
    assert nk % NA_V_CHUNK == 0 and ((rows - NA_K_ROWS) * GRID_W) % NA_V_CHUNK == 0
    v_t = _values_transposed(v, 0, HEAD_DIM, NA_HEADS, HEAD_DIM, NA_V_CHUNK)
    vc_t = _values_transposed(cv, 0, HEAD_DIM, NA_HEADS, HEAD_DIM, c)[:, 0]
    dva = HEAD_DIM + FLASH_ONES_ROWS

    def variant(rb):
        return jnp.where(rb == 0, 0, jnp.where(rb == n_blk - 1, 2, 1))

    hps = NA_HEADS_PER_STEP
    gw = hps * HEAD_DIM
    n_grp = NA_HEADS // hps
    return pl.pallas_call(
        functools.partial(_na_body, rows=rows),
        grid=(n_grp, n_blk),
        in_specs=[pl.BlockSpec(memory_space=pltpu.SMEM),
                  pl.BlockSpec((nq, gw), lambda hg, rb: (rb, hg)),
                  pl.BlockSpec((s, gw), lambda hg, rb: (0, n_grp + hg)),
                  pl.BlockSpec((hps, n_vc, dva, NA_V_CHUNK), lambda hg, rb: (hg, 0, 0, 0)),
                  pl.BlockSpec((c, gw), lambda hg, rb: (0, hg)),
                  pl.BlockSpec((hps, dva, c), lambda hg, rb: (hg, 0, 0)),
                  pl.BlockSpec((hps, 1, nk, nq), lambda hg, rb: (hg, variant(rb), 0, 0)),
                  pl.BlockSpec(memory_space=pl.ANY)],
        out_specs=pl.BlockSpec((nq, gw), lambda hg, rb: (rb, hg)),
        out_shape=jax.ShapeDtypeStruct(o_buf.shape, o_buf.dtype),
        input_output_aliases={7: 0},
        compiler_params=_params("parallel", "arbitrary"),
        name="neighbourhood_attention",
    )(bound, qk, qk, v_t, ck, vc_t, bias, o_buf)


def _dwconv_body(prev_ref, cur_ref, next_ref, w_ref, bdw_ref, g_ref, b_ref, o_ref, win_ref, conv_ref,
                 *, n_blk):
    i = pl.program_id(0)
    tm, d = cur_ref.shape
    nc = d // LANES
    has_prev = i > 0
    has_next = i < n_blk - 1
    for c in range(nc):
        sl = slice(c * LANES, (c + 1) * LANES)
        win_ref[c, 0:CONV_HALO, :] = jnp.where(has_prev, prev_ref[:, sl].astype(F32), 0.0)
        win_ref[c, CONV_HALO:CONV_HALO + tm, :] = cur_ref[:, sl].astype(F32)
        win_ref[c, CONV_HALO + tm:, :] = jnp.where(has_next, next_ref[:, sl].astype(F32), 0.0)

    groups = (CONV_WIDTH + SUBLANES) // SUBLANES
    span = tm + SUBLANES * (groups - 1)

    def chunk(c, carry):
        w = w_ref[c]
        acc = jnp.zeros((tm, LANES), F32) + bdw_ref[c]
        for b in range(SUBLANES):
            shifted = win_ref[c, pl.ds(b, span), :]
            for a in range(groups):
                o = SUBLANES * a + b
                if 1 <= o <= CONV_WIDTH:
                    acc = acc + shifted[SUBLANES * a:SUBLANES * a + tm, :] * w[o - 1:o, :]
        conv_ref[c] = acc
        return carry

    lax.fori_loop(0, nc, chunk, 0)

    total = conv_ref[0]
    for c in range(1, nc):
        total = total + conv_ref[c]
    mu = jnp.sum(total, axis=-1, keepdims=True) * (1.0 / d)
    sq = jnp.zeros((tm, LANES), F32)
    for c in range(nc):
        dlt = conv_ref[c] - mu
        sq = sq + dlt * dlt
    rstd = lax.rsqrt(jnp.sum(sq, axis=-1, keepdims=True) * (1.0 / d) + LN_EPS)
    for c in range(nc):
        y = (conv_ref[c] - mu) * rstd * g_ref[c] + b_ref[c]
        o_ref[:, c * LANES:(c + 1) * LANES] = (y * jax.nn.sigmoid(y)).astype(o_ref.dtype)


def _dwconv_ln_silu(u, w_dw, b_dw, g_ln, b_ln):
    m, d = u.shape
    tm = _pick(m, 128, CONV_HALO)
    nc = d // LANES
    n_blk = m // tm
    per = tm // CONV_HALO
    last_halo = m // CONV_HALO - 1
    assert CONV_WIDTH // 2 < CONV_HALO and CONV_WIDTH + 1 <= 2 * CONV_HALO
    vec = lambda v: v.reshape(nc, 1, LANES).astype(F32)
    w_chunks = jnp.transpose(w_dw.astype(F32).reshape(CONV_WIDTH, nc, LANES), (1, 0, 2))
    whole = lambda shape: pl.BlockSpec(shape, lambda i: (0,) * len(shape))
    return pl.pallas_call(
        functools.partial(_dwconv_body, n_blk=n_blk),
        grid=(n_blk,),
        in_specs=[pl.BlockSpec((CONV_HALO, d), lambda i: (jnp.maximum(i * per - 1, 0), 0)),
                  pl.BlockSpec((tm, d), lambda i: (i, 0)),
                  pl.BlockSpec((CONV_HALO, d), lambda i: (jnp.minimum((i + 1) * per, last_halo), 0)),
                  whole((nc, CONV_WIDTH, LANES)), whole((nc, 1, LANES)), whole((nc, 1, LANES)),
                  whole((nc, 1, LANES))],
        out_specs=pl.BlockSpec((tm, d), lambda i: (i, 0)),
        out_shape=jax.ShapeDtypeStruct((m, d), BF16),
        scratch_shapes=[pltpu.VMEM((nc, tm + 2 * CONV_HALO, LANES), F32),
                        pltpu.VMEM((nc, tm, LANES), F32)],
        compiler_params=_params("arbitrary"),
        name="dwconv_ln_silu",
    )(u, u, u, w_chunks, vec(b_dw), vec(g_ln), vec(b_ln))


def _rope_tables(n_ctx, n_tok):
    t = jnp.arange(n_tok, dtype=jnp.int32)
    pos = jnp.stack([t // GRID_W, t % GRID_W], axis=-1).astype(F32)
    n_freq = MLA_ROPE // 4
    inv_freq = ROPE_THETA ** (-jnp.arange(n_freq, dtype=F32) / n_freq)
    ang = (pos[:, :, None] * inv_freq).reshape(n_tok, 2 * n_freq)
    cos, sin = jnp.cos(ang), jnp.sin(ang)
    zeros = jnp.zeros((n_tok, LANES - MLA_ROPE), F32)
    cos_lat = jnp.concatenate([cos, cos, zeros], axis=-1)
    sin_lat = jnp.concatenate([-sin, sin, zeros], axis=-1)
    cos_ctx = jnp.concatenate([jnp.ones((n_ctx, MLA_ROPE), F32), jnp.zeros((n_ctx, LANES - MLA_ROPE), F32)], -1)
    sin_ctx = jnp.zeros((n_ctx, LANES), F32)
    return (jnp.concatenate([cos_ctx, cos_lat], 0), jnp.concatenate([sin_ctx, sin_lat], 0))


def _attention_layer(x, xc, mod, layer, j, p):
    s, d = x.shape
    c = xc.shape[0]
    rows = s // GRID_W
    assert s % GRID_W == 0 and rows % NA_Q_ROWS == 0 and rows >= NA_K_ROWS
    w_in = functools.partial(_matmul, w=jnp.swapaxes(p["att_w_in"], 1, 2), w_layer=j, w_transposed=True)

    hl = _rms_norm(x, p["g_mix"][layer], mod, layer, 0, 0, 1)
    hc = _rms_norm(xc, p["g_mix"][layer], mod, layer, 1, 0, 1)

    g_q = jnp.tile(p["att_g_na_q"][j], NA_HEADS) * (HEAD_DIM ** -0.5 * LOG2E)
    g_k = jnp.tile(p["att_g_na_k"][j], NA_HEADS)

    qk = w_in(hl, off=0, n=2 * NA_WIDTH, kind="headnorm", gain=jnp.concatenate([g_q, g_k]))
    na_v = w_in(hl, off=OFF_NA_V, n=NA_WIDTH)
    q_lat = w_in(hl, off=OFF_Q_LAT, n=MLA_Q_RANK)
    kv_lat = w_in(hl, off=OFF_KV_LAT, n=MLA_KV_RANK)
    k_rope = w_in(hl, off=OFF_K_ROPE, n=MLA_ROPE, kind="pad_lanes")
    c_k = w_in(hc, off=OFF_NA_K, n=NA_WIDTH, kind="headnorm", gain=g_k)
    c_v = w_in(hc, off=OFF_NA_V, n=NA_WIDTH)
    c_kv_lat = w_in(hc, off=OFF_KV_LAT, n=MLA_KV_RANK)
    c_k_rope = w_in(hc, off=OFF_K_ROPE, n=MLA_ROPE, kind="pad_lanes")

    w_qb = p["att_w_qb"][j].reshape(MLA_Q_RANK, MLA_HEADS, MLA_QK)
    w_qb_rope = jnp.pad(w_qb[:, :, MLA_NOPE:], ((0, 0), (0, 0), (0, LANES - MLA_ROPE)))
    w_qb_perm = jnp.concatenate([w_qb[:, :, :MLA_NOPE].reshape(MLA_Q_RANK, -1),
                                 w_qb_rope.reshape(MLA_Q_RANK, -1)], axis=-1)

    q_raw = _matmul(_rms_norm(q_lat, p["att_g_qa"][j]), w_qb_perm, tn_pref=1024)
    kv_n = _rms_norm(jnp.concatenate([c_kv_lat, kv_lat], axis=0), p["att_g_kva"][j])
    kv_raw = _matmul(kv_n, p["att_w_kvb"], w_layer=j, tn_pref=4096)
    kv_head = MLA_NOPE + MLA_V
    kr_all = jnp.concatenate([c_k_rope, k_rope], axis=0)

    cos, sin = _rope_tables(c, s)
    q_full = _mla_heads(q_raw, 0, MLA_NOPE, q_raw, MLA_HEADS * MLA_NOPE, False, p["att_g_mla_q"][j],
                        cos[c:], sin[c:], MLA_QK ** -0.5 * LOG2E)
    k_full = _mla_heads(kv_raw, 0, kv_head, kr_all, 0, True, p["att_g_mla_k"][j], cos, sin, 1.0)
    tkc = _pick(c + s, FLASH_TK, LANES)
    v_t = _values_transposed(kv_raw, MLA_NOPE, kv_head, MLA_HEADS, MLA_V, tkc)

    bias = _na_bias_table(p["att_na_rpb"][j], rows)
    o = jnp.zeros((s, NA_WIDTH + MLA_HEADS * MLA_V), BF16)
    rpb_log2 = p["att_na_rpb"][j] * LOG2E
    qk_cap = (HEAD_DIM * HEAD_DIM ** -0.5 * LOG2E * BF16_NORM_SLACK) * (
        jnp.max(jnp.abs(p["att_g_na_q"][j])) * jnp.max(jnp.abs(p["att_g_na_k"][j])))
    rpb_hi = jnp.maximum(jnp.max(rpb_log2), 0.0)
    rpb_lo = jnp.minimum(jnp.min(rpb_log2), 0.0)
    na_bound = jnp.stack([qk_cap + rpb_hi, 2.0 * qk_cap + rpb_hi - rpb_lo]).astype(F32)
    o = _na_attention(qk, na_v, c_k, c_v, bias, rows, o, na_bound)
    bound = (MLA_QK * MLA_QK ** -0.5 * LOG2E * BF16_NORM_SLACK) * (
        jnp.max(jnp.abs(p["att_g_mla_q"][j])) * jnp.max(jnp.abs(p["att_g_mla_k"][j])))
    o = _mla_attention(q_full, k_full, v_t, o, NA_WIDTH, bound.reshape(1).astype(F32))
    return _matmul(o, p["att_w_out"], w_layer=j, kind="resid", out_dtype=F32, res=x, gate=mod,
                   gate_layer=layer, gate_row=0, gate_chunk=2)


def _conv_layer(x, mod, layer, j, p):
    d = x.shape[1]
    hl = _rms_norm(x, p["g_mix"][layer], mod, layer, 0, 0, 1)
    b1 = p["conv_b_pw1"][j]
    u = _matmul(hl, p["conv_w_pw1"], w_layer=j, off=0, off2=d, n=d, kind="glu", bias=b1[:d],
                bias2=b1[d:], tn_pref=256)
    v = _dwconv_ln_silu(u, p["conv_w_dw"][j], p["conv_b_dw"][j], p["conv_g_ln"][j], p["conv_b_ln"][j])
    return _matmul(v, p["conv_w_pw2"], w_layer=j, kind="resid", out_dtype=F32, bias=p["conv_b_pw2"][j],
                   res=x, gate=mod, gate_layer=layer, gate_row=0, gate_chunk=2)


def _mlp(x, mod, layer, p):
    h = _rms_norm(x, p["g_mlp"][layer], mod, layer, 0, 3, 4)
    u = _matmul(h, p["mlp_w1"], w_layer=layer, kind="relu2")
    return _matmul(u, p["mlp_w2"], w_layer=layer, kind="resid", out_dtype=F32, res=x, gate=mod,
                   gate_layer=layer, gate_row=0, gate_chunk=5, tm_pref=2048, tn_pref=1024, tk_pref=1024)


def kernel(x, c, ctx, c_ctx, ada_w, ada_b, g_mix, g_mlp, mlp_w1, mlp_w2, att_w_in, att_g_qa, att_w_qb, att_g_kva, att_w_kvb, att_g_na_q, att_g_na_k, att_na_rpb, att_g_mla_q, att_g_mla_k, att_w_out, conv_w_pw1, conv_b_pw1, conv_w_dw, conv_b_dw, conv_g_ln, conv_b_ln, conv_w_pw2, conv_b_pw2):
    p = dict(g_mix=g_mix, g_mlp=g_mlp, mlp_w1=mlp_w1, mlp_w2=mlp_w2, att_w_in=att_w_in,
             att_g_qa=att_g_qa, att_w_qb=att_w_qb, att_g_kva=att_g_kva, att_w_kvb=att_w_kvb,
             att_g_na_q=att_g_na_q, att_g_na_k=att_g_na_k, att_na_rpb=att_na_rpb,
             att_g_mla_q=att_g_mla_q, att_g_mla_k=att_g_mla_k, att_w_out=att_w_out,
             conv_w_pw1=conv_w_pw1, conv_b_pw1=conv_b_pw1, conv_w_dw=conv_w_dw, conv_b_dw=conv_b_dw,
             conv_g_ln=conv_g_ln, conv_b_ln=conv_b_ln, conv_w_pw2=conv_w_pw2, conv_b_pw2=conv_b_pw2)
    batch, _, d = x.shape
    depth = ada_w.shape[0]
    assert batch == 1 and c.shape[0] == 1
    svec = jnp.zeros((MOD_ROWS, d), F32).at[0].set(c[0]).at[1].set(c_ctx)
    mod = _modulation(svec, ada_w, ada_b)

    xl = x[0]
    xc = ctx[0]
    for layer in range(depth):
        j = layer // 2
        if layer % 2 == 0:
            assert not any(m % 2 == 0 for m in range(layer + 1, depth))
            xl = _attention_layer(xl, xc, mod, layer, j, p)
        else:
            xl = _conv_layer(xl, mod, layer, j, p)
        xl = _mlp(xl, mod, layer, p)
    return xl[None]
```
